```python
import math
import jax, jax.numpy as jnp
from jax import lax
import numpy as np

D_MODEL = 1024
BATCH = 1
SEQ = 16384
DEPTH = 2

RMS_EPS = 1e-6
LN_EPS = 1e-5
N_EVEN = (DEPTH + 1) // 2
N_ODD = DEPTH // 2
CONV_CH = D_MODEL // 2
CONV_WIDTH = 31
GDN_HEADS = 4
GDN_DK = 128
GDN_DV = 128
GDN_KEY = GDN_HEADS * GDN_DK
GDN_VAL = GDN_HEADS * GDN_DV
SHORT_CONV = 4
CHUNK = 64
IN0_COLS = 2 * CONV_CH + 2 * GDN_KEY + 2 * GDN_VAL + 2 * GDN_HEADS
SB_HEADS = 16
SB_HD = D_MODEL // SB_HEADS
SB_BLOCK = 128
D_FF = 2816
FFN_CONV = 3

kernel_name = 'hybrid_conformer_gdn_stickbreak_convffn'


def rmsnorm(x, w):
    xf = x.astype(jnp.float32)
    y = xf * lax.rsqrt(jnp.mean(xf * xf, axis=-1, keepdims=True) + RMS_EPS)
    return (y * w.astype(jnp.float32)).astype(x.dtype)


def layernorm(x, w, b):
    xf = x.astype(jnp.float32)
    mu = jnp.mean(xf, axis=-1, keepdims=True)
    xc = xf - mu
    y = xc * lax.rsqrt(jnp.mean(xc * xc, axis=-1, keepdims=True) + LN_EPS)
    return (y * w.astype(jnp.float32) + b.astype(jnp.float32)).astype(x.dtype)


def l2norm(x):
    return x * lax.rsqrt(jnp.sum(x * x, axis=-1, keepdims=True) + RMS_EPS)


def causal_dwconv(x, w):
    K, C = w.shape
    return lax.conv_general_dilated(
        x, w[:, None, :], window_strides=(1,), padding=[(K - 1, 0)],
        dimension_numbers=('NWC', 'WIO', 'NWC'), feature_group_count=C)


def conformer_conv(a_val, a_gate, dw_w, dw_b, ln_w, ln_b):
    h = a_val * jax.nn.sigmoid(a_gate)
    h = causal_dwconv(h, dw_w) + dw_b
    h = layernorm(h, ln_w, ln_b)
    return jax.nn.silu(h)


def chunk_gated_delta_rule(q, k, v, g, beta):
    B, L, H, DK = q.shape
    DV = v.shape[-1]
    N = L // CHUNK

    def to_chunks(t):
        return t.reshape(B, N, CHUNK, H, -1).transpose(1, 0, 3, 2, 4)

    q_c, k_c, v_c = to_chunks(q), to_chunks(k), to_chunks(v)
    g_c = g.reshape(B, N, CHUNK, H).transpose(1, 0, 3, 2)
    b_c = beta.reshape(B, N, CHUNK, H).transpose(1, 0, 3, 2)
    g_cum = jnp.cumsum(g_c, axis=-1)
    idx = jnp.arange(CHUNK)
    incl = idx[:, None] >= idx[None, :]
    strict = idx[:, None] > idx[None, :]
    decay = jnp.exp(jnp.where(incl, g_cum[..., :, None] - g_cum[..., None, :], -jnp.inf))
    k_beta = k_c * b_c[..., None]
    v_beta = v_c * b_c[..., None]
    kk = jnp.einsum('nbhid,nbhjd->nbhij', k_beta, k_c) * decay
    lower = jnp.where(strict, kk, 0.0)
    u = lax.linalg.triangular_solve(lower, v_beta, left_side=True, lower=True, unit_diagonal=True)
    w = lax.linalg.triangular_solve(lower, k_beta * jnp.exp(g_cum)[..., None],
                                    left_side=True, lower=True, unit_diagonal=True)
    qk = jnp.einsum('nbhid,nbhjd->nbhij', q_c, k_c) * decay
    g_last = g_cum[..., -1]
    k_tail = k_c * jnp.exp(g_last[..., None] - g_cum)[..., None]
    q_dec = q_c * jnp.exp(g_cum)[..., None]

    def step(S, inp):
        qd, qk_i, u_i, w_i, kt, gl = inp
        v_new = u_i - jnp.einsum('bhcd,bhde->bhce', w_i, S)
        o = jnp.einsum('bhcd,bhde->bhce', qd, S) + jnp.einsum('bhij,bhje->bhie', qk_i, v_new)
        S = S * jnp.exp(gl)[..., None, None] + jnp.einsum('bhcd,bhce->bhde', kt, v_new)
        return S, o

    S0 = jnp.zeros((B, H, DK, DV), jnp.float32)
    _, o = lax.scan(step, S0, (q_dec, qk, u, w, k_tail, g_last))
    return o.transpose(1, 0, 3, 2, 4).reshape(B, L, H, DV)


def gated_deltanet(q, k, v, z, a, b, conv_w, a_log, dt_bias, onorm_w):
    Bsz, L, _ = q.shape
    dtype = q.dtype
    qkv = jax.nn.silu(causal_dwconv(jnp.concatenate([q, k, v], axis=-1), conv_w))
    qkv = qkv.astype(jnp.float32)
    qf = qkv[..., :GDN_KEY].reshape(Bsz, L, GDN_HEADS, GDN_DK)
    kf = qkv[..., GDN_KEY:2 * GDN_KEY].reshape(Bsz, L, GDN_HEADS, GDN_DK)
    vf = qkv[..., 2 * GDN_KEY:].reshape(Bsz, L, GDN_HEADS, GDN_DV)
    qf = l2norm(qf) * (GDN_DK ** -0.5)
    kf = l2norm(kf)
    beta = jax.nn.sigmoid(b.astype(jnp.float32))
    g = -jnp.exp(a_log.astype(jnp.float32)) * jax.nn.softplus(
        a.astype(jnp.float32) + dt_bias.astype(jnp.float32))
    o = chunk_gated_delta_rule(qf, kf, vf, g, beta)
    o = rmsnorm(o, onorm_w) * jax.nn.silu(z.astype(jnp.float32).reshape(Bsz, L, GDN_HEADS, GDN_DV))
    return o.reshape(Bsz, L, GDN_VAL).astype(dtype)


def conv_deltanet_mixer(h, w_in, a_dw_w, a_dw_b, a_ln_w, a_ln_b,
                        gdn_conv_w, gdn_a_log, gdn_dt_bias, gdn_onorm_w, w_out):
    p = h @ w_in
    c0 = CONV_CH
    c1 = c0 + CONV_CH
    c2 = c1 + GDN_KEY
    c3 = c2 + GDN_KEY
    c4 = c3 + GDN_VAL
    c5 = c4 + GDN_VAL
    c6 = c5 + GDN_HEADS
    a_val, a_gate, q, k, v, z, a, b = jnp.split(p, [c0, c1, c2, c3, c4, c5, c6], axis=-1)
    y_a = conformer_conv(a_val, a_gate, a_dw_w, a_dw_b, a_ln_w, a_ln_b)
    y_b = gated_deltanet(q, k, v, z, a, b, gdn_conv_w, gdn_a_log, gdn_dt_bias, gdn_onorm_w)
    return jnp.concatenate([y_a, y_b], axis=-1) @ w_out


def stick_breaking_attention(q, k, v):
    B, L, H, D = q.shape
    dtype = q.dtype
    nb = L // SB_BLOCK
    qf = q.astype(jnp.float32) * (D ** -0.5)
    kf = k.astype(jnp.float32)
    vf = v.astype(jnp.float32)
    qb = qf.reshape(B, nb, SB_BLOCK, H, D).transpose(1, 0, 3, 2, 4)
    kpos = jnp.arange(L)

    def block(args):
        qi, i = args
        z = jnp.einsum('bhqd,bkhd->bhqk', qi, kf)
        qpos = i * SB_BLOCK + jnp.arange(SB_BLOCK)
        causal = kpos[None, :] < qpos[:, None]
        neg_log_1mb = jnp.where(causal, jax.nn.softplus(z), 0.0)
        tail = lax.cumsum(neg_log_1mb, axis=3, reverse=True) - neg_log_1mb
        log_a = -jax.nn.softplus(-z) - tail
        att = jnp.where(causal, jnp.exp(log_a), 0.0)
        return jnp.einsum('bhqk,bkhd->bqhd', att, vf)

    o = lax.map(block, (qb, jnp.arange(nb)))
    return o.transpose(1, 0, 2, 3, 4).reshape(B, L, H * D).astype(dtype)


def stick_breaking_mixer(h, w_qkv, w_out):
    B, L, _ = h.shape
    qkv = (h @ w_qkv).reshape(B, L, 3, SB_HEADS, SB_HD)
    o = stick_breaking_attention(qkv[:, :, 0], qkv[:, :, 1], qkv[:, :, 2])
    return o @ w_out


def conv_ffn(h, w_up, conv_w, w_down):
    u = causal_dwconv(h @ w_up, conv_w)
    gate, val = jnp.split(u, 2, axis=-1)
    return (jax.nn.silu(gate) * val) @ w_down


def setup_inputs(seed: int = 0) -> dict:
    key = jax.random.key(seed)
    ks = jax.random.split(key, 24)
    nrm = jax.random.normal
    f32 = jnp.float32
    x = nrm(ks[0], (BATCH, SEQ, D_MODEL), f32)
    mix0_norm = 1.0 + 0.02 * nrm(ks[1], (N_EVEN, D_MODEL), f32)
    w_in0 = nrm(ks[2], (N_EVEN, D_MODEL, IN0_COLS), f32) * D_MODEL ** -0.5
    a_dw_w = nrm(ks[3], (N_EVEN, CONV_WIDTH, CONV_CH), f32) * CONV_WIDTH ** -0.5
    a_dw_b = 0.02 * nrm(ks[4], (N_EVEN, CONV_CH), f32)
    a_ln_w = 1.0 + 0.02 * nrm(ks[5], (N_EVEN, CONV_CH), f32)
    a_ln_b = 0.02 * nrm(ks[6], (N_EVEN, CONV_CH), f32)
    gdn_conv_w = nrm(ks[7], (N_EVEN, SHORT_CONV, 2 * GDN_KEY + GDN_VAL), f32) * SHORT_CONV ** -0.5
    gdn_a_log = jnp.log(jax.random.uniform(ks[8], (N_EVEN, GDN_HEADS), f32, 1.0, 16.0))
    dt = jnp.exp(jax.random.uniform(ks[9], (N_EVEN, GDN_HEADS), f32)
                 * (math.log(0.1) - math.log(0.001)) + math.log(0.001))
    gdn_dt_bias = dt + jnp.log(-jnp.expm1(-dt))
    gdn_onorm_w = 1.0 + 0.02 * nrm(ks[10], (N_EVEN, GDN_DV), f32)
    w_out0 = nrm(ks[11], (N_EVEN, CONV_CH + GDN_VAL, D_MODEL), f32) * (CONV_CH + GDN_VAL) ** -0.5
    mix1_norm = 1.0 + 0.02 * nrm(ks[12], (N_ODD, D_MODEL), f32)
    w_qkv1 = nrm(ks[13], (N_ODD, D_MODEL, 3 * SB_HEADS * SB_HD), f32) * D_MODEL ** -0.5
    w_out1 = nrm(ks[14], (N_ODD, SB_HEADS * SB_HD, D_MODEL), f32) * (SB_HEADS * SB_HD) ** -0.5
    ffn_norm = 1.0 + 0.02 * nrm(ks[15], (DEPTH, D_MODEL), f32)
    w_up = nrm(ks[16], (DEPTH, D_MODEL, 2 * D_FF), f32) * D_MODEL ** -0.5
    ffn_conv_w = nrm(ks[17], (DEPTH, FFN_CONV, 2 * D_FF), f32) * FFN_CONV ** -0.5
    w_down = nrm(ks[18], (DEPTH, D_FF, D_MODEL), f32) * D_FF ** -0.5
    final_norm = 1.0 + 0.02 * nrm(ks[19], (D_MODEL,), f32)
    return {'x': x, 'mix0_norm': mix0_norm, 'w_in0': w_in0, 'a_dw_w': a_dw_w, 'a_dw_b': a_dw_b,
            'a_ln_w': a_ln_w, 'a_ln_b': a_ln_b, 'gdn_conv_w': gdn_conv_w, 'gdn_a_log': gdn_a_log,
            'gdn_dt_bias': gdn_dt_bias, 'gdn_onorm_w': gdn_onorm_w, 'w_out0': w_out0,
            'mix1_norm': mix1_norm, 'w_qkv1': w_qkv1, 'w_out1': w_out1,
            'ffn_norm': ffn_norm, 'w_up': w_up, 'ffn_conv_w': ffn_conv_w, 'w_down': w_down,
            'final_norm': final_norm}


def reference(x, mix0_norm, w_in0, a_dw_w, a_dw_b, a_ln_w, a_ln_b, gdn_conv_w, gdn_a_log,
              gdn_dt_bias, gdn_onorm_w, w_out0, mix1_norm, w_qkv1, w_out1,
              ffn_norm, w_up, ffn_conv_w, w_down, final_norm):
    for layer in range(DEPTH):
        j = layer // 2
        if layer % 2 == 0:
            x = x + conv_deltanet_mixer(rmsnorm(x, mix0_norm[j]), w_in0[j], a_dw_w[j], a_dw_b[j],
                                        a_ln_w[j], a_ln_b[j], gdn_conv_w[j], gdn_a_log[j],
                                        gdn_dt_bias[j], gdn_onorm_w[j], w_out0[j])
        else:
            x = x + stick_breaking_mixer(rmsnorm(x, mix1_norm[j]), w_qkv1[j], w_out1[j])
        x = x + conv_ffn(rmsnorm(x, ffn_norm[layer]), w_up[layer], ffn_conv_w[layer], w_down[layer])
    return rmsnorm(x, final_norm)
```

```python
import functools
import math

import jax
import jax.numpy as jnp
from jax import lax
from jax.experimental import pallas as pl
from jax.experimental.pallas import tpu as pltpu

F32 = jnp.float32
BF16 = jnp.bfloat16

D_MODEL = 1024
RMS_EPS = 1e-6
LN_EPS = 1e-5
CONV_CH = 512
CONV_WIDTH = 31
GDN_HEADS = 4
GDN_DK = 128
GDN_KEY = GDN_HEADS * GDN_DK
SHORT_CONV = 4
SB_HEADS = 16
SB_HD = 64
D_FF = 2816
FFN_CONV = 3

LANES = 128
SUBLANES = 8
VMEM_LIMIT = 56 * 1024 * 1024

IN0_PAD = 3200
FF_CHUNK = 256
N_FF_CHUNKS = D_FF // FF_CHUNK
GDN_CHUNK = 128
CONF_HALO = 32
CONF_SUB = 32
ATT_TQ = 256
ATT_TK = 256
LOG2E = 1.4426950408889634
ATT_STOP = 160.0


def _cparams(*sem):
    return pltpu.CompilerParams(dimension_semantics=sem, vmem_limit_bytes=VMEM_LIMIT)


def _bdot(a, b):
    return jnp.dot(a.astype(BF16), b.astype(BF16), preferred_element_type=F32)


def _bdot_nt(a, b):
    return lax.dot_general(a.astype(BF16), b.astype(BF16), (((1,), (1,)), ((), ())),
                           preferred_element_type=F32)


def _rms_matmul_kernel(x_ref, nw_ref, w_ref, cs_ref, o_ref, h_ref):
    @pl.when(pl.program_id(1) == 0)
    def _():
        x = x_ref[...]
        ms = jnp.mean(x * x, axis=-1, keepdims=True)
        h_ref[...] = (x * lax.rsqrt(ms + RMS_EPS) * nw_ref[...]).astype(BF16)

    acc = jnp.dot(h_ref[...], w_ref[...], preferred_element_type=F32)
    o_ref[...] = (acc * cs_ref[...]).astype(o_ref.dtype)


def _rms_matmul(x, nw, w, colscale, out_dtype, tm, tn):
    L, D = x.shape
    N = w.shape[1]
    return pl.pallas_call(
        _rms_matmul_kernel,
        grid=(L // tm, N // tn),
        in_specs=[
            pl.BlockSpec((tm, D), lambda i, j: (i, 0)),
            pl.BlockSpec((1, D), lambda i, j: (0, 0)),
            pl.BlockSpec((D, tn), lambda i, j: (0, j)),
            pl.BlockSpec((1, tn), lambda i, j: (0, j)),
        ],
        out_specs=pl.BlockSpec((tm, tn), lambda i, j: (i, j)),
        out_shape=jax.ShapeDtypeStruct((L, N), out_dtype),
        scratch_shapes=[pltpu.VMEM((tm, D), BF16)],
        compiler_params=_cparams("arbitrary", "arbitrary"),
        name="rms_matmul",
    )(x, nw, w, colscale)


def _proj_residual_kernel(ya_ref, yb_ref, wa_ref, wb_ref, res_ref, o_ref):
    acc = jnp.dot(ya_ref[...], wa_ref[...], preferred_element_type=F32)
    acc = acc + jnp.dot(yb_ref[...], wb_ref[...], preferred_element_type=F32)
    o_ref[...] = res_ref[...] + acc


def _proj_residual(ya, ca, yb, cb, w, res, tm):
    L, D = res.shape
    kh = w.shape[0] // 2
    return pl.pallas_call(
        _proj_residual_kernel,
        grid=(L // tm,),
        in_specs=[
            pl.BlockSpec((tm, kh), lambda i: (i, ca)),
            pl.BlockSpec((tm, kh), lambda i: (i, cb)),
            pl.BlockSpec((kh, D), lambda i: (0, 0)),
            pl.BlockSpec((kh, D), lambda i: (1, 0)),
            pl.BlockSpec((tm, D), lambda i: (i, 0)),
        ],
        out_specs=pl.BlockSpec((tm, D), lambda i: (i, 0)),
        out_shape=jax.ShapeDtypeStruct((L, D), F32),
        compiler_params=_cparams("arbitrary"),
        name="proj_residual",
    )(ya, yb, w, w, res)


def _conformer_kernel(val_ref, gate_ref, w_ref, b_ref, lnw_ref, lnb_ref, o_ref, ext_ref, sh_ref):
    tm = val_ref.shape[0]
    i = pl.program_id(0)

    @pl.when(i == 0)
    def _():
        ext_ref[0:CONF_HALO, :] = jnp.zeros((CONF_HALO, CONV_CH), F32)

    @pl.when(i > 0)
    def _():
        ext_ref[0:CONF_HALO, :] = ext_ref[tm:tm + CONF_HALO, :]

    ext_ref[CONF_HALO:, :] = val_ref[...] * jax.nn.sigmoid(gate_ref[...])

    nrows = tm + CONF_HALO - SUBLANES
    for b in range(1, SUBLANES):
        sh_ref[b, 0:nrows, :] = ext_ref[pl.ds(b, nrows), :]

    first = CONF_HALO - (CONV_WIDTH - 1)

    def body(r, carry):
        base = pl.multiple_of(r * CONF_SUB, CONF_SUB)
        acc = jnp.zeros((CONF_SUB, CONV_CH), F32)
        for k in range(CONV_WIDTH):
            a, b = divmod(first + k, SUBLANES)
            src = ext_ref if b == 0 else sh_ref.at[b]
            acc = acc + src[pl.ds(base + SUBLANES * a, CONF_SUB), :] * w_ref[k:k + 1, :]
        acc = acc + b_ref[...]
        mu = jnp.mean(acc, axis=-1, keepdims=True)
        xc = acc - mu
        var = jnp.mean(xc * xc, axis=-1, keepdims=True)
        y = xc * lax.rsqrt(var + LN_EPS) * lnw_ref[...] + lnb_ref[...]
        o_ref[pl.ds(base, CONF_SUB), :] = (y * jax.nn.sigmoid(y)).astype(o_ref.dtype)
        return carry

    lax.fori_loop(0, tm // CONF_SUB, body, 0)


def _conformer(p, dw_w, dw_b, ln_w, ln_b, tm):
    L = p.shape[0]
    row = lambda i: (0, 0)
    return pl.pallas_call(
        _conformer_kernel,
        grid=(L // tm,),
        in_specs=[
            pl.BlockSpec((tm, CONV_CH), lambda i: (i, 0)),
            pl.BlockSpec((tm, CONV_CH), lambda i: (i, 1)),
            pl.BlockSpec((CONF_HALO, CONV_CH), row),
            pl.BlockSpec((1, CONV_CH), row),
            pl.BlockSpec((1, CONV_CH), row),
            pl.BlockSpec((1, CONV_CH), row),
        ],
        out_specs=pl.BlockSpec((tm, CONV_CH), lambda i: (i, 0)),
        out_shape=jax.ShapeDtypeStruct((L, CONV_CH), BF16),
        scratch_shapes=[pltpu.VMEM((tm + CONF_HALO, CONV_CH), F32),
                        pltpu.VMEM((SUBLANES, tm + CONF_HALO, CONV_CH), F32)],
        compiler_params=_cparams("arbitrary"),
        name="conformer_conv",
    )(p, p, dw_w, dw_b, ln_w, ln_b)


def _unit_lower_inverse(low, row, col):
    n = low.shape[0]
    eye = (row == col).astype(F32)

    def same_block(b):
        sh = b.bit_length() - 1
        return jnp.right_shift(row, sh) == jnp.right_shift(col, sh)

    base = SUBLANES
    a = jnp.where(same_block(base), low, 0.0)
    a2 = _bdot(a, a)
    a4 = _bdot(a2, a2)
    ia = eye - a
    p1 = ia + _bdot(ia, a2)
    t = p1 + _bdot(p1, a4)
    b = base
    while b < n:
        c = jnp.where(same_block(2 * b) & jnp.logical_not(same_block(b)), low, 0.0)
        t = t - _bdot(_bdot(t, c), t)
        b *= 2
    return t


def _gdn_kernel(q_ref, k_ref, v_ref, z_ref, ab_ref, cw_ref, alog_ref, dtb_ref, onw_ref,
                o_ref, ext_ref, s_ref, qkv_ref, gc_ref, gct_ref, beta_ref):
    tb = q_ref.shape[0]
    nc = tb // GDN_CHUNK
    c_ = GDN_CHUNK
    i = pl.program_id(0)

    @pl.when(i == 0)
    def _():
        ext_ref[0:SUBLANES, :] = jnp.zeros((SUBLANES, 3 * GDN_KEY), F32)
        s_ref[...] = jnp.zeros(s_ref.shape, F32)

    @pl.when(i > 0)
    def _():
        ext_ref[0:SUBLANES, :] = ext_ref[tb:tb + SUBLANES, :]

    ext_ref[SUBLANES:, 0:GDN_KEY] = q_ref[...]
    ext_ref[SUBLANES:, GDN_KEY:2 * GDN_KEY] = k_ref[...]
    ext_ref[SUBLANES:, 2 * GDN_KEY:] = v_ref[...]

    first = SUBLANES - (SHORT_CONV - 1)
    conv = ext_ref[pl.ds(first, tb), :] * cw_ref[0:1, :]
    for t in range(1, SHORT_CONV):
        conv = conv + ext_ref[pl.ds(first + t, tb), :] * cw_ref[t:t + 1, :]
    qkv = conv * jax.nn.sigmoid(conv)

    ab = ab_ref[...]
    g = -jnp.exp(alog_ref[...]) * jax.nn.softplus(ab + dtb_ref[...])
    beta = jax.nn.sigmoid(ab)

    rr = lax.broadcasted_iota(jnp.int32, (tb, tb), 0)
    cc = lax.broadcasted_iota(jnp.int32, (tb, tb), 1)
    csh = c_.bit_length() - 1
    tri = ((rr >= cc) & (jnp.right_shift(rr, csh) == jnp.right_shift(cc, csh))).astype(F32)
    gc = jnp.dot(tri, g, preferred_element_type=F32, precision=lax.Precision.HIGHEST)
    qkv_ref[...] = qkv
    gc_ref[...] = gc
    gct_ref[...] = gc.T
    beta_ref[...] = beta

    row = lax.broadcasted_iota(jnp.int32, (c_, c_), 0)
    col = lax.broadcasted_iota(jnp.int32, (c_, c_), 1)
    incl = row >= col
    strict = row > col

    for c in range(nc):
        r0 = c * c_
        for h in range(GDN_HEADS):
            q = qkv_ref[r0:r0 + c_, h * GDN_DK:(h + 1) * GDN_DK]
            k = qkv_ref[r0:r0 + c_, GDN_KEY + h * GDN_DK:GDN_KEY + (h + 1) * GDN_DK]
            v = qkv_ref[r0:r0 + c_, 2 * GDN_KEY + h * GDN_DK:2 * GDN_KEY + (h + 1) * GDN_DK]
            q = q * lax.rsqrt(jnp.sum(q * q, axis=-1, keepdims=True) + RMS_EPS) * (GDN_DK ** -0.5)
            k = k * lax.rsqrt(jnp.sum(k * k, axis=-1, keepdims=True) + RMS_EPS)
            gcol = gc_ref[r0:r0 + c_, h:h + 1]
            grow = gct_ref[h:h + 1, r0:r0 + c_]
            bcol = beta_ref[r0:r0 + c_, GDN_HEADS + h:GDN_HEADS + h + 1]
            glast = gcol[c_ - 1:c_, :]

            decay = jnp.where(incl, jnp.exp(gcol - grow), 0.0)
            egc = jnp.exp(gcol)
            kb = k * bcol
            vb = v * bcol
            low = jnp.where(strict, _bdot_nt(kb, k) * decay, 0.0)
            tinv = _unit_lower_inverse(low, row, col)
            u = _bdot(tinv, vb)
            w = _bdot(tinv, kb * egc)
            qk = _bdot_nt(q, k) * decay
            ktt = (k * jnp.exp(glast - gcol)).T
            qd = q * egc

            s = s_ref[h]
            v_new = u - _bdot(w, s)
            o = _bdot(qd, s) + _bdot(qk, v_new)
            s_ref[h] = s * jnp.exp(glast) + _bdot(ktt, v_new)

            zh = z_ref[r0:r0 + c_, h * GDN_DK:(h + 1) * GDN_DK]
            on = o * lax.rsqrt(jnp.mean(o * o, axis=-1, keepdims=True) + RMS_EPS) * onw_ref[...]
            o_ref[r0:r0 + c_, h * GDN_DK:(h + 1) * GDN_DK] = (
                on * (zh * jax.nn.sigmoid(zh))).astype(o_ref.dtype)


def _gdn(p, conv_w, alog_row, dtb_row, onorm_row, tb):
    L = p.shape[0]
    row = lambda i: (0, 0)
    kblk = GDN_KEY // LANES
    return pl.pallas_call(
        _gdn_kernel,
        grid=(L // tb,),
        in_specs=[
            pl.BlockSpec((tb, GDN_KEY), lambda i: (i, 2)),
            pl.BlockSpec((tb, GDN_KEY), lambda i: (i, 3)),
            pl.BlockSpec((tb, GDN_KEY), lambda i: (i, 4)),
            pl.BlockSpec((tb, GDN_KEY), lambda i: (i, 5)),
            pl.BlockSpec((tb, LANES), lambda i: (i, 6 * kblk)),
            pl.BlockSpec((SUBLANES, 3 * GDN_KEY), row),
            pl.BlockSpec((1, LANES), row),
            pl.BlockSpec((1, LANES), row),
            pl.BlockSpec((1, LANES), row),
        ],
        out_specs=pl.BlockSpec((tb, GDN_KEY), lambda i: (i, 0)),
        out_shape=jax.ShapeDtypeStruct((L, GDN_KEY), BF16),
        scratch_shapes=[pltpu.VMEM((tb + SUBLANES, 3 * GDN_KEY), F32),
                        pltpu.VMEM((GDN_HEADS, GDN_DK, GDN_DK), F32),
                        pltpu.VMEM((tb, 3 * GDN_KEY), F32),
                        pltpu.VMEM((tb, LANES), F32),
                        pltpu.VMEM((LANES, tb), F32),
                        pltpu.VMEM((tb, LANES), F32)],
        compiler_params=_cparams("arbitrary"),
        name="gated_deltanet",
    )(p, p, p, p, p, conv_w, alog_row, dtb_row, onorm_row)


def _ffn_kernel(x_ref, nw_ref, wg_ref, wv_ref, cw_ref, wd_ref, fnw_ref, o_ref,
                h_ref, acc_ref, ubuf_ref, carry_ref, *, final_norm):
    tm = x_ref.shape[0]
    i = pl.program_id(0)

    @pl.when(i == 0)
    def _():
        carry_ref[...] = jnp.zeros(carry_ref.shape, F32)

    x = x_ref[...]
    ms = jnp.mean(x * x, axis=-1, keepdims=True)
    h_ref[...] = (x * lax.rsqrt(ms + RMS_EPS) * nw_ref[...]).astype(BF16)
    acc_ref[...] = jnp.zeros(acc_ref.shape, F32)

    def body(c, carry):
        h = h_ref[...]
        cw = cw_ref[c]
        convs = []
        for gv, w_ref in enumerate((wg_ref, wv_ref)):
            u = jnp.dot(h, w_ref[c], preferred_element_type=F32)
            ubuf_ref[gv, 0:SUBLANES, :] = carry_ref[c, gv]
            ubuf_ref[gv, SUBLANES:, :] = u
            carry_ref[c, gv] = u[tm - SUBLANES:, :]
            convs.append(ubuf_ref[gv, pl.ds(SUBLANES - 2, tm), :] * cw[gv, 0:1, :]
                         + ubuf_ref[gv, pl.ds(SUBLANES - 1, tm), :] * cw[gv, 1:2, :]
                         + u * cw[gv, 2:3, :])
        gate, val = convs
        act = (gate * jax.nn.sigmoid(gate) * val).astype(BF16)
        acc_ref[...] += jnp.dot(act, wd_ref[c], preferred_element_type=F32)
        return carry

    lax.fori_loop(0, N_FF_CHUNKS, body, 0)
    y = x + acc_ref[...]
    if final_norm:
        ms2 = jnp.mean(y * y, axis=-1, keepdims=True)
        y = y * lax.rsqrt(ms2 + RMS_EPS) * fnw_ref[...]
    o_ref[...] = y


def _ffn(x, nw, wg, wv, cw, wd, fnw, tm, final_norm):
    L, D = x.shape
    const3 = lambda i: (0, 0, 0)
    resident = dict(pipeline_mode=pl.Buffered(1))
    return pl.pallas_call(
        functools.partial(_ffn_kernel, final_norm=final_norm),
        grid=(L // tm,),
        in_specs=[
            pl.BlockSpec((tm, D), lambda i: (i, 0)),
            pl.BlockSpec((1, D), lambda i: (0, 0)),
            pl.BlockSpec((N_FF_CHUNKS, D, FF_CHUNK), const3, **resident),
            pl.BlockSpec((N_FF_CHUNKS, D, FF_CHUNK), const3, **resident),
            pl.BlockSpec((N_FF_CHUNKS, 2, SUBLANES, FF_CHUNK), lambda i: (0, 0, 0, 0)),
            pl.BlockSpec((N_FF_CHUNKS, FF_CHUNK, D), const3, **resident),
            pl.BlockSpec((1, D), lambda i: (0, 0)),
        ],
        out_specs=pl.BlockSpec((tm, D), lambda i: (i, 0)),
        out_shape=jax.ShapeDtypeStruct((L, D), F32),
        scratch_shapes=[pltpu.VMEM((tm, D), BF16),
                        pltpu.VMEM((tm, D), F32),
                        pltpu.VMEM((2, tm + SUBLANES, FF_CHUNK), F32),
                        pltpu.VMEM((N_FF_CHUNKS, 2, SUBLANES, FF_CHUNK), F32)],
        compiler_params=_cparams("arbitrary"),
        name="conv_ffn",
    )(x, nw, wg, wv, cw, wd, fnw)


def _softplus2(z):
    return jnp.where(z > 30.0, z, jnp.log2(1.0 + jnp.exp2(jnp.minimum(z, 30.0))))


def _attn_kernel(q_ref, k_ref, v_ref, u_ref, o_ref):
    tq = q_ref.shape[0]
    tk = u_ref.shape[0]
    qi = pl.program_id(1)
    n_diag = tq // tk
    q = q_ref[...]
    lane = lax.broadcasted_iota(jnp.int32, (1, LANES), 1)
    umat = u_ref[...]
    rowq = lax.broadcasted_iota(jnp.int32, (tq, tk), 0)
    colk = lax.broadcasted_iota(jnp.int32, (tq, tk), 1)

    def tile(qh, kb, run, acc, diag_offset):
        start = pl.multiple_of(kb * tk, tk)
        k = k_ref[pl.ds(start, tk), :]
        v = v_ref[pl.ds(start, tk), :]
        z = lax.dot_general(qh, k, (((1,), (1,)), ((), ())), preferred_element_type=F32)
        sp = _softplus2(z)
        if diag_offset is not None:
            valid = (colk + diag_offset) < rowq
            sp = jnp.where(valid, sp, 0.0)
        sp_hi = sp.astype(BF16)
        sp_lo = (sp - sp_hi.astype(F32)).astype(BF16)
        cum = (jnp.dot(sp_hi, umat, preferred_element_type=F32)
               + jnp.dot(sp_lo, umat, preferred_element_type=F32))
        att = jnp.exp2(z - cum - run)
        if diag_offset is not None:
            att = jnp.where(valid, att, 0.0)
        acc = acc + jnp.dot(att.astype(BF16), v, preferred_element_type=F32)
        return run + cum[:, 0:1], acc

    outs = []
    for head in range(2):
        qh = jnp.where((lane >= head * SB_HD) & (lane < (head + 1) * SB_HD), q, jnp.zeros_like(q))
        run = jnp.zeros((tq, 1), F32)
        acc = jnp.zeros((tq, LANES), F32)
        for d in range(n_diag):
            dd = n_diag - 1 - d
            run, acc = tile(qh, qi * n_diag + dd, run, acc, dd * tk)

        def cond(state):
            kb, run, _ = state
            return (kb >= 0) & (jnp.min(run) < ATT_STOP)

        def body(state):
            kb, run, acc = state
            run, acc = tile(qh, kb, run, acc, None)
            return kb - 1, run, acc

        _, _, acc = lax.while_loop(cond, body, (qi * n_diag - 1, run, acc))
        outs.append(acc)

    o_ref[...] = jnp.where(lane < SB_HD, outs[0], outs[1]).astype(o_ref.dtype)


def _attention(qkv, umat, tq):
    L = qkv.shape[0]
    n_pairs = SB_HEADS * SB_HD // LANES
    tk = umat.shape[0]
    return pl.pallas_call(
        _attn_kernel,
        grid=(n_pairs, L // tq),
        in_specs=[
            pl.BlockSpec((tq, LANES), lambda hp, i: (i, hp)),
            pl.BlockSpec((L, LANES), lambda hp, i: (0, n_pairs + hp)),
            pl.BlockSpec((L, LANES), lambda hp, i: (0, 2 * n_pairs + hp)),
            pl.BlockSpec((tk, tk), lambda hp, i: (0, 0)),
        ],
        out_specs=pl.BlockSpec((tq, LANES), lambda hp, i: (i, hp)),
        out_shape=jax.ShapeDtypeStruct((L, SB_HEADS * SB_HD), BF16),
        compiler_params=_cparams("arbitrary", "arbitrary"),
        name="stickbreak_attention",
    )(qkv, qkv, qkv, umat)


def _row(v, width=None):
    v = v.astype(F32).reshape(1, -1)
    if width is not None and v.shape[1] < width:
        v = jnp.pad(v, ((0, 0), (0, width - v.shape[1])))
    return v


def _pad_rows(w, rows):
    return jnp.pad(w.astype(F32), ((0, rows - w.shape[0]), (0, 0)))


def _ffn_weights(w_up, conv_w, w_down):
    d = w_up.shape[0]
    wu = w_up.astype(BF16).reshape(d, 2, N_FF_CHUNKS, FF_CHUNK).transpose(1, 2, 0, 3)
    cw = _pad_rows(conv_w, SUBLANES).reshape(SUBLANES, 2, N_FF_CHUNKS, FF_CHUNK).transpose(2, 1, 0, 3)
    wd = w_down.astype(BF16).reshape(N_FF_CHUNKS, FF_CHUNK, d)
    return wu[0], wu[1], cw, wd


def kernel(x, mix0_norm, w_in0, a_dw_w, a_dw_b, a_ln_w, a_ln_b, gdn_conv_w, gdn_a_log,
           gdn_dt_bias, gdn_onorm_w, w_out0, mix1_norm, w_qkv1, w_out1,
           ffn_norm, w_up, ffn_conv_w, w_down, final_norm):
    B, L, D = x.shape
    assert B == 1 and D == D_MODEL
    xs = x.reshape(L, D)

    w_in = jnp.pad(w_in0[0], ((0, 0), (0, IN0_PAD - w_in0.shape[2]))).astype(BF16)
    ones_in = jnp.ones((1, IN0_PAD), F32)
    p = _rms_matmul(xs, _row(mix0_norm[0]), w_in, ones_in, F32, tm=512, tn=640)
    y_a = _conformer(p, _pad_rows(a_dw_w[0], CONF_HALO), _row(a_dw_b[0]), _row(a_ln_w[0]),
                     _row(a_ln_b[0]), tm=512)
    y_b = _gdn(p, _pad_rows(gdn_conv_w[0], SUBLANES), _row(gdn_a_log[0], LANES),
               _row(gdn_dt_bias[0], LANES), _row(gdn_onorm_w[0]), tb=2 * GDN_CHUNK)
    xs = _proj_residual(y_a, 0, y_b, 0, w_out0[0].astype(BF16), xs, tm=512)
    wg, wv, cw, wd = _ffn_weights(w_up[0], ffn_conv_w[0], w_down[0])
    xs = _ffn(xs, _row(ffn_norm[0]), wg, wv, cw, wd, _row(final_norm), tm=512, final_norm=False)

    hd = SB_HEADS * SB_HD
    qscale = jnp.concatenate([jnp.full((1, hd), SB_HD ** -0.5 * LOG2E, F32),
                              jnp.ones((1, 2 * hd), F32)], axis=1)
    qkv = _rms_matmul(xs, _row(mix1_norm[0]), w_qkv1[0].astype(BF16), qscale, BF16, tm=512, tn=768)
    ki = lax.broadcasted_iota(jnp.int32, (ATT_TK, ATT_TK), 0)
    si = lax.broadcasted_iota(jnp.int32, (ATT_TK, ATT_TK), 1)
    umat = (ki >= si).astype(BF16)
    o = _attention(qkv, umat, tq=ATT_TQ)
    xs = _proj_residual(o, 0, o, 1, w_out1[0].astype(BF16), xs, tm=512)
    wg, wv, cw, wd = _ffn_weights(w_up[1], ffn_conv_w[1], w_down[1])
    xs = _ffn(xs, _row(ffn_norm[1]), wg, wv, cw, wd, _row(final_norm), tm=512, final_norm=True)
    return xs.reshape(B, L, D)
```

```python
import functools
import math

import jax
import jax.numpy as jnp
from jax import lax
from jax.experimental import pallas as pl
from jax.experimental.pallas import tpu as pltpu

F32 = jnp.float32
BF16 = jnp.bfloat16

D_MODEL = 1024
RMS_EPS = 1e-6
LN_EPS = 1e-5
CONV_CH = 512
CONV_WIDTH = 31
GDN_HEADS = 4
GDN_DK = 128
GDN_KEY = GDN_HEADS * GDN_DK
SHORT_CONV = 4
SB_HEADS = 16
SB_HD = 64
D_FF = 2816
FFN_CONV = 3

LANES = 128
SUBLANES = 8
VMEM_LIMIT = 56 * 1024 * 1024

IN0_MAIN = 3072
FF_CHUNK = 256
N_FF_CHUNKS = D_FF // FF_CHUNK
GDN_CHUNK = 128
CONF_HALO = 32
CONF_SUB = 32
ATT_TQ = 256
ATT_TK = 256
ATT_PAIRS = 2
LOG2E = 1.4426950408889634
ATT_STOP = 160.0


def _cparams(*sem):
    return pltpu.CompilerParams(dimension_semantics=sem, vmem_limit_bytes=VMEM_LIMIT)


def _bdot(a, b):
    return jnp.dot(a.astype(BF16), b.astype(BF16), preferred_element_type=F32)


def _bdot_nt(a, b):
    return lax.dot_general(a.astype(BF16), b.astype(BF16), (((1,), (1,)), ((), ())),
                           preferred_element_type=F32)


def _rms_matmul_kernel(*refs, has_side):
    if has_side:
        x_ref, nw_ref, w_ref, cs_ref, ws_ref, o_ref, os_ref, h_ref = refs
    else:
        x_ref, nw_ref, w_ref, cs_ref, o_ref, h_ref = refs

    @pl.when(pl.program_id(1) == 0)
    def _():
        x = x_ref[...]
        ms = jnp.mean(x * x, axis=-1, keepdims=True)
        h = (x * lax.rsqrt(ms + RMS_EPS) * nw_ref[...]).astype(BF16)
        h_ref[...] = h
        if has_side:
            os_ref[...] = jnp.dot(h, ws_ref[...], preferred_element_type=F32)

    acc = jnp.dot(h_ref[...], w_ref[...], preferred_element_type=F32)
    o_ref[...] = (acc * cs_ref[...]).astype(o_ref.dtype)


def _rms_matmul(x, nw, w, colscale, out_dtype, tm, tn, w_side=None):
    L, D = x.shape
    N = w.shape[1]
    has_side = w_side is not None
    in_specs = [
        pl.BlockSpec((tm, D), lambda i, j: (i, 0)),
        pl.BlockSpec((1, D), lambda i, j: (0, 0)),
        pl.BlockSpec((D, tn), lambda i, j: (0, j)),
        pl.BlockSpec((1, tn), lambda i, j: (0, j)),
    ]
    out_specs = [pl.BlockSpec((tm, tn), lambda i, j: (i, j))]
    out_shape = [jax.ShapeDtypeStruct((L, N), out_dtype)]
    args = [x, nw, w, colscale]
    if has_side:
        ns = w_side.shape[1]
        in_specs.append(pl.BlockSpec((D, ns), lambda i, j: (0, 0)))
        out_specs.append(pl.BlockSpec((tm, ns), lambda i, j: (i, 0)))
        out_shape.append(jax.ShapeDtypeStruct((L, ns), F32))
        args.append(w_side)
    outs = pl.pallas_call(
        functools.partial(_rms_matmul_kernel, has_side=has_side),
        grid=(L // tm, N // tn),
        in_specs=in_specs,
        out_specs=out_specs,
        out_shape=out_shape,
        scratch_shapes=[pltpu.VMEM((tm, D), BF16)],
        compiler_params=_cparams("arbitrary", "arbitrary"),
        name="rms_matmul",
    )(*args)
    return outs if has_side else outs[0]


def _proj_residual_kernel(ya_ref, yb_ref, wa_ref, wb_ref, res_ref, o_ref):
    acc = jnp.dot(ya_ref[...], wa_ref[...], preferred_element_type=F32)
    acc = acc + jnp.dot(yb_ref[...], wb_ref[...], preferred_element_type=F32)
    o_ref[...] = res_ref[...] + acc


def _proj_residual(ya, ca, yb, cb, w, res, tm):
    L, D = res.shape
    kh = w.shape[0] // 2
    return pl.pallas_call(
        _proj_residual_kernel,
        grid=(L // tm,),
        in_specs=[
            pl.BlockSpec((tm, kh), lambda i: (i, ca)),
            pl.BlockSpec((tm, kh), lambda i: (i, cb)),
            pl.BlockSpec((kh, D), lambda i: (0, 0)),
            pl.BlockSpec((kh, D), lambda i: (1, 0)),
            pl.BlockSpec((tm, D), lambda i: (i, 0)),
        ],
        out_specs=pl.BlockSpec((tm, D), lambda i: (i, 0)),
        out_shape=jax.ShapeDtypeStruct((L, D), F32),
        compiler_params=_cparams("arbitrary"),
        name="proj_residual",
    )(ya, yb, w, w, res)


def _conformer_kernel(val_ref, gate_ref, w_ref, b_ref, lnw_ref, lnb_ref, o_ref, ext_ref, sh_ref):
    tm = val_ref.shape[0]
    i = pl.program_id(0)

    @pl.when(i == 0)
    def _():
        ext_ref[0:CONF_HALO, :] = jnp.zeros((CONF_HALO, CONV_CH), F32)

    @pl.when(i > 0)
    def _():
        ext_ref[0:CONF_HALO, :] = ext_ref[tm:tm + CONF_HALO, :]

    ext_ref[CONF_HALO:, :] = val_ref[...].astype(F32) * jax.nn.sigmoid(gate_ref[...].astype(F32))

    nrows = tm + CONF_HALO - SUBLANES
    for b in range(1, SUBLANES):
        sh_ref[b, 0:nrows, :] = ext_ref[pl.ds(b, nrows), :]

    first = CONF_HALO - (CONV_WIDTH - 1)

    def body(r, carry):
        base = pl.multiple_of(r * CONF_SUB, CONF_SUB)
        acc = jnp.zeros((CONF_SUB, CONV_CH), F32)
        for k in range(CONV_WIDTH):
            a, b = divmod(first + k, SUBLANES)
            src = ext_ref if b == 0 else sh_ref.at[b]
            acc = acc + src[pl.ds(base + SUBLANES * a, CONF_SUB), :] * w_ref[k:k + 1, :]
        acc = acc + b_ref[...]
        mu = jnp.mean(acc, axis=-1, keepdims=True)
        xc = acc - mu
        var = jnp.mean(xc * xc, axis=-1, keepdims=True)
        y = xc * lax.rsqrt(var + LN_EPS) * lnw_ref[...] + lnb_ref[...]
        o_ref[pl.ds(base, CONF_SUB), :] = (y * jax.nn.sigmoid(y)).astype(o_ref.dtype)
        return carry

    lax.fori_loop(0, tm // CONF_SUB, body, 0)


def _conformer(p, dw_w, dw_b, ln_w, ln_b, tm):
    L = p.shape[0]
    row = lambda i: (0, 0)
    return pl.pallas_call(
        _conformer_kernel,
        grid=(L // tm,),
        in_specs=[
            pl.BlockSpec((tm, CONV_CH), lambda i: (i, 0)),
            pl.BlockSpec((tm, CONV_CH), lambda i: (i, 1)),
            pl.BlockSpec((CONF_HALO, CONV_CH), row),
            pl.BlockSpec((1, CONV_CH), row),
            pl.BlockSpec((1, CONV_CH), row),
            pl.BlockSpec((1, CONV_CH), row),
        ],
        out_specs=pl.BlockSpec((tm, CONV_CH), lambda i: (i, 0)),
        out_shape=jax.ShapeDtypeStruct((L, CONV_CH), BF16),
        scratch_shapes=[pltpu.VMEM((tm + CONF_HALO, CONV_CH), F32),
                        pltpu.VMEM((SUBLANES, tm + CONF_HALO, CONV_CH), F32)],
        compiler_params=_cparams("arbitrary"),
        name="conformer_conv",
    )(p, p, dw_w, dw_b, ln_w, ln_b)


def _bmm(a, b):
    return lax.dot_general(a.astype(BF16), b.astype(BF16), (((2,), (1,)), ((0,), (0,))),
                           preferred_element_type=F32)


def _bmm_nt(a, b):
    return lax.dot_general(a.astype(BF16), b.astype(BF16), (((2,), (2,)), ((0,), (0,))),
                           preferred_element_type=F32)


def _unit_lower_inverse(low, row, col):
    n = low.shape[-1]
    eye = (row == col).astype(F32)

    def same_block(b):
        sh = b.bit_length() - 1
        return jnp.right_shift(row, sh) == jnp.right_shift(col, sh)

    base = SUBLANES
    a = jnp.where(same_block(base), low, 0.0)
    a2 = _bmm(a, a)
    a4 = _bmm(a2, a2)
    ia = eye - a
    p1 = ia + _bmm(ia, a2)
    t = p1 + _bmm(p1, a4)
    b = base
    while b < n:
        c = jnp.where(same_block(2 * b) & jnp.logical_not(same_block(b)), low, 0.0)
        t = t - _bmm(_bmm(t, c), t)
        b *= 2
    return t


def _gdn_kernel(q_ref, k_ref, v_ref, z_ref, ab_ref, cw_ref, alog_ref, dtb_ref, onw_ref,
                o_ref, ext_ref, s_ref, qkv_ref, gc_ref, gct_ref, beta_ref):
    tb = q_ref.shape[0]
    nc = tb // GDN_CHUNK
    c_ = GDN_CHUNK
    i = pl.program_id(0)

    @pl.when(i == 0)
    def _():
        ext_ref[0:SUBLANES, :] = jnp.zeros((SUBLANES, 3 * GDN_KEY), F32)
        s_ref[...] = jnp.zeros(s_ref.shape, F32)

    @pl.when(i > 0)
    def _():
        ext_ref[0:SUBLANES, :] = ext_ref[tb:tb + SUBLANES, :]

    ext_ref[SUBLANES:, 0:GDN_KEY] = q_ref[...].astype(F32)
    ext_ref[SUBLANES:, GDN_KEY:2 * GDN_KEY] = k_ref[...].astype(F32)
    ext_ref[SUBLANES:, 2 * GDN_KEY:] = v_ref[...].astype(F32)

    first = SUBLANES - (SHORT_CONV - 1)
    conv = ext_ref[pl.ds(first, tb), :] * cw_ref[0:1, :]
    for t in range(1, SHORT_CONV):
        conv = conv + ext_ref[pl.ds(first + t, tb), :] * cw_ref[t:t + 1, :]
    qkv = conv * jax.nn.sigmoid(conv)

    ab = ab_ref[...]
    g = -jnp.exp(alog_ref[...]) * jax.nn.softplus(ab + dtb_ref[...])
    beta = jax.nn.sigmoid(ab)

    rr = lax.broadcasted_iota(jnp.int32, (tb, tb), 0)
    cc = lax.broadcasted_iota(jnp.int32, (tb, tb), 1)
    csh = c_.bit_length() - 1
    tri = ((rr >= cc) & (jnp.right_shift(rr, csh) == jnp.right_shift(cc, csh))).astype(F32)
    gc = jnp.dot(tri, g, preferred_element_type=F32, precision=lax.Precision.HIGHEST)
    qkv_ref[...] = qkv
    gc_ref[...] = gc
    gct_ref[...] = gc.T
    beta_ref[...] = beta

    row = lax.broadcasted_iota(jnp.int32, (c_, c_), 0)
    col = lax.broadcasted_iota(jnp.int32, (c_, c_), 1)
    incl = row >= col
    strict = row > col

    pairs = [(c, h) for c in range(nc) for h in range(GDN_HEADS)]

    def stack(fn):
        return jnp.stack([fn(c * c_, h) for c, h in pairs])

    q = stack(lambda r0, h: qkv_ref[r0:r0 + c_, h * GDN_DK:(h + 1) * GDN_DK])
    k = stack(lambda r0, h: qkv_ref[r0:r0 + c_, GDN_KEY + h * GDN_DK:GDN_KEY + (h + 1) * GDN_DK])
    v = stack(lambda r0, h: qkv_ref[r0:r0 + c_, 2 * GDN_KEY + h * GDN_DK:2 * GDN_KEY + (h + 1) * GDN_DK])
    gcol = stack(lambda r0, h: gc_ref[r0:r0 + c_, h:h + 1])
    grow = stack(lambda r0, h: gct_ref[h:h + 1, r0:r0 + c_])
    bcol = stack(lambda r0, h: beta_ref[r0:r0 + c_, GDN_HEADS + h:GDN_HEADS + h + 1])
    glast = gcol[:, c_ - 1:c_, :]

    q = q * lax.rsqrt(jnp.sum(q * q, axis=-1, keepdims=True) + RMS_EPS) * (GDN_DK ** -0.5)
    k = k * lax.rsqrt(jnp.sum(k * k, axis=-1, keepdims=True) + RMS_EPS)
    decay = jnp.where(incl, jnp.exp(gcol - grow), 0.0)
    egc = jnp.exp(gcol)
    kb = k * bcol
    low = jnp.where(strict, _bmm_nt(kb, k) * decay, 0.0)
    tinv = _unit_lower_inverse(low, row, col)
    uw = _bmm(tinv, jnp.concatenate([v * bcol, kb * egc], axis=-1))
    qk = _bmm_nt(q, k) * decay
    ktt = jnp.swapaxes(k * jnp.exp(glast - gcol), 1, 2)
    lhs_state = jnp.concatenate([uw[:, :, GDN_DK:], q * egc], axis=1)
    lhs_vnew = jnp.concatenate([qk, ktt], axis=1)
    u = uw[:, :, :GDN_DK]
    sdecay = jnp.exp(glast)

    s = s_ref[...]
    for c in range(nc):
        sl = slice(c * GDN_HEADS, (c + 1) * GDN_HEADS)
        from_state = _bmm(lhs_state[sl], s)
        v_new = u[sl] - from_state[:, :c_]
        from_vnew = _bmm(lhs_vnew[sl], v_new)
        o = from_state[:, c_:] + from_vnew[:, :c_]
        s = s * sdecay[sl] + from_vnew[:, c_:]
        on = o * lax.rsqrt(jnp.mean(o * o, axis=-1, keepdims=True) + RMS_EPS) * onw_ref[...]
        r0 = c * c_
        for h in range(GDN_HEADS):
            zh = z_ref[r0:r0 + c_, h * GDN_DK:(h + 1) * GDN_DK].astype(F32)
            o_ref[r0:r0 + c_, h * GDN_DK:(h + 1) * GDN_DK] = (
                on[h] * (zh * jax.nn.sigmoid(zh))).astype(o_ref.dtype)
    s_ref[...] = s


def _gdn(p, ab, conv_w, alog_row, dtb_row, onorm_row, tb):
    L = p.shape[0]
    row = lambda i: (0, 0)
    return pl.pallas_call(
        _gdn_kernel,
        grid=(L // tb,),
        in_specs=[
            pl.BlockSpec((tb, GDN_KEY), lambda i: (i, 2)),
            pl.BlockSpec((tb, GDN_KEY), lambda i: (i, 3)),
            pl.BlockSpec((tb, GDN_KEY), lambda i: (i, 4)),
            pl.BlockSpec((tb, GDN_KEY), lambda i: (i, 5)),
            pl.BlockSpec((tb, LANES), lambda i: (i, 0)),
            pl.BlockSpec((SUBLANES, 3 * GDN_KEY), row),
            pl.BlockSpec((1, LANES), row),
            pl.BlockSpec((1, LANES), row),
            pl.BlockSpec((1, LANES), row),
        ],
        out_specs=pl.BlockSpec((tb, GDN_KEY), lambda i: (i, 0)),
        out_shape=jax.ShapeDtypeStruct((L, GDN_KEY), BF16),
        scratch_shapes=[pltpu.VMEM((tb + SUBLANES, 3 * GDN_KEY), F32),
                        pltpu.VMEM((GDN_HEADS, GDN_DK, GDN_DK), F32),
                        pltpu.VMEM((tb, 3 * GDN_KEY), F32),
                        pltpu.VMEM((tb, LANES), F32),
                        pltpu.VMEM((LANES, tb), F32),
                        pltpu.VMEM((tb, LANES), F32)],
        compiler_params=_cparams("arbitrary"),
        name="gated_deltanet",
    )(p, p, p, p, ab, conv_w, alog_row, dtb_row, onorm_row)


def _ffn_kernel(x_ref, nw_ref, wg_ref, wv_ref, cw_ref, wd_ref, fnw_ref, o_ref,
                h_ref, acc_ref, ubuf_a, ubuf_b, carry_ref, *, final_norm):
    tm = x_ref.shape[0]
    i = pl.program_id(0)

    @pl.when(i == 0)
    def _():
        carry_ref[...] = jnp.zeros(carry_ref.shape, F32)

    x = x_ref[...]
    ms = jnp.mean(x * x, axis=-1, keepdims=True)
    h_ref[...] = (x * lax.rsqrt(ms + RMS_EPS) * nw_ref[...]).astype(BF16)
    acc_ref[...] = jnp.zeros(acc_ref.shape, F32)

    def up(c, ubuf):
        h = h_ref[...]
        for gv, w_ref in enumerate((wg_ref, wv_ref)):
            u = jnp.dot(h, w_ref[c], preferred_element_type=F32)
            ubuf[gv, 0:SUBLANES, :] = carry_ref[c, gv]
            ubuf[gv, SUBLANES:, :] = u
            carry_ref[c, gv] = u[tm - SUBLANES:, :]

    def down(c, ubuf):
        cw = cw_ref[c]
        gate, val = [
            functools.reduce(jnp.add, [
                ubuf[gv, pl.ds(SUBLANES - (FFN_CONV - 1) + t, tm), :] * cw[gv, t:t + 1, :]
                for t in range(FFN_CONV)])
            for gv in range(2)]
        act = (gate * jax.nn.sigmoid(gate) * val).astype(BF16)
        acc_ref[...] += jnp.dot(act, wd_ref[c], preferred_element_type=F32)

    up(0, ubuf_a)

    def body(j, carry):
        c = 2 * j
        up(c + 1, ubuf_b)
        down(c, ubuf_a)
        up(c + 2, ubuf_a)
        down(c + 1, ubuf_b)
        return carry

    assert N_FF_CHUNKS % 2 == 1
    lax.fori_loop(0, N_FF_CHUNKS // 2, body, 0)
    down(N_FF_CHUNKS - 1, ubuf_a)
    y = x + acc_ref[...]
    if final_norm:
        ms2 = jnp.mean(y * y, axis=-1, keepdims=True)
        y = y * lax.rsqrt(ms2 + RMS_EPS) * fnw_ref[...]
    o_ref[...] = y


def _ffn(x, nw, wg, wv, cw, wd, fnw, tm, final_norm):
    L, D = x.shape
    const3 = lambda i: (0, 0, 0)
    resident = dict(pipeline_mode=pl.Buffered(1))
    return pl.pallas_call(
        functools.partial(_ffn_kernel, final_norm=final_norm),
        grid=(L // tm,),
        in_specs=[
            pl.BlockSpec((tm, D), lambda i: (i, 0)),
            pl.BlockSpec((1, D), lambda i: (0, 0)),
            pl.BlockSpec((N_FF_CHUNKS, D, FF_CHUNK), const3, **resident),
            pl.BlockSpec((N_FF_CHUNKS, D, FF_CHUNK), const3, **resident),
            pl.BlockSpec((N_FF_CHUNKS, 2, SUBLANES, FF_CHUNK), lambda i: (0, 0, 0, 0)),
            pl.BlockSpec((N_FF_CHUNKS, FF_CHUNK, D), const3, **resident),
            pl.BlockSpec((1, D), lambda i: (0, 0)),
        ],
        out_specs=pl.BlockSpec((tm, D), lambda i: (i, 0)),
        out_shape=jax.ShapeDtypeStruct((L, D), F32),
        scratch_shapes=[pltpu.VMEM((tm, D), BF16),
                        pltpu.VMEM((tm, D), F32),
                        pltpu.VMEM((2, tm + SUBLANES, FF_CHUNK), F32),
                        pltpu.VMEM((2, tm + SUBLANES, FF_CHUNK), F32),
                        pltpu.VMEM((N_FF_CHUNKS, 2, SUBLANES, FF_CHUNK), F32)],
        compiler_params=_cparams("arbitrary"),
        name="conv_ffn",
    )(x, nw, wg, wv, cw, wd, fnw)


def _softplus2(z):
    return jnp.where(z > 30.0, z, jnp.log2(1.0 + jnp.exp2(jnp.minimum(z, 30.0))))


def _attn_kernel(q_ref, k_ref, v_ref, u_ref, o_ref):
    tq = q_ref.shape[0]
    tk = u_ref.shape[0]
    qi = pl.program_id(1)
    n_diag = tq // tk
    n_pair = q_ref.shape[1] // LANES
    lane = lax.broadcasted_iota(jnp.int32, (1, LANES), 1)
    umat = u_ref[...]
    rowq = lax.broadcasted_iota(jnp.int32, (tq, tk), 0)
    colk = lax.broadcasted_iota(jnp.int32, (tq, tk), 1)

    qhs = []
    for p in range(n_pair):
        q = q_ref[:, p * LANES:(p + 1) * LANES]
        for head in range(2):
            own = (lane >= head * SB_HD) & (lane < (head + 1) * SB_HD)
            qhs.append(jnp.where(own, q, jnp.zeros_like(q)))
    n_chain = len(qhs)

    def tile(kb, runs, accs, diag_offset):
        start = pl.multiple_of(kb * tk, tk)
        new_runs, new_accs = [], []
        for c in range(n_chain):
            p = c // 2
            k = k_ref[pl.ds(start, tk), p * LANES:(p + 1) * LANES]
            v = v_ref[pl.ds(start, tk), p * LANES:(p + 1) * LANES]
            z = lax.dot_general(qhs[c], k, (((1,), (1,)), ((), ())), preferred_element_type=F32)
            sp = _softplus2(z)
            if diag_offset is not None:
                valid = (colk + diag_offset) < rowq
                sp = jnp.where(valid, sp, 0.0)
            cum = jnp.dot(sp.astype(BF16), umat, preferred_element_type=F32)
            att = jnp.exp2(z - cum - runs[c])
            if diag_offset is not None:
                att = jnp.where(valid, att, 0.0)
            new_accs.append(accs[c] + jnp.dot(att.astype(BF16), v, preferred_element_type=F32))
            new_runs.append(runs[c] + cum[:, 0:1])
        return tuple(new_runs), tuple(new_accs)

    runs = tuple(jnp.zeros((tq, 1), F32) for _ in range(n_chain))
    accs = tuple(jnp.zeros((tq, LANES), F32) for _ in range(n_chain))
    for d in range(n_diag):
        dd = n_diag - 1 - d
        runs, accs = tile(qi * n_diag + dd, runs, accs, dd * tk)

    def cond(state):
        kb, runs, _ = state
        least = functools.reduce(jnp.minimum, runs)
        return (kb >= 0) & (jnp.min(least) < ATT_STOP)

    def body(state):
        kb, runs, accs = state
        runs, accs = tile(kb, runs, accs, None)
        return kb - 1, runs, accs

    _, _, accs = lax.while_loop(cond, body, (qi * n_diag - 1, runs, accs))
    for p in range(n_pair):
        o_ref[:, p * LANES:(p + 1) * LANES] = jnp.where(
            lane < SB_HD, accs[2 * p], accs[2 * p + 1]).astype(o_ref.dtype)


def _attention(qkv, umat, tq, pairs_per_step):
    L = qkv.shape[0]
    n_pairs = SB_HEADS * SB_HD // LANES
    n_steps = n_pairs // pairs_per_step
    width = pairs_per_step * LANES
    tk = umat.shape[0]
    return pl.pallas_call(
        _attn_kernel,
        grid=(n_steps, L // tq),
        in_specs=[
            pl.BlockSpec((tq, width), lambda hp, i: (i, hp)),
            pl.BlockSpec((L, width), lambda hp, i: (0, n_steps + hp)),
            pl.BlockSpec((L, width), lambda hp, i: (0, 2 * n_steps + hp)),
            pl.BlockSpec((tk, tk), lambda hp, i: (0, 0)),
        ],
        out_specs=pl.BlockSpec((tq, width), lambda hp, i: (i, hp)),
        out_shape=jax.ShapeDtypeStruct((L, SB_HEADS * SB_HD), BF16),
        compiler_params=_cparams("arbitrary", "arbitrary"),
        name="stickbreak_attention",
    )(qkv, qkv, qkv, umat)


def _row(v, width=None):
    v = v.astype(F32).reshape(1, -1)
    if width is not None and v.shape[1] < width:
        v = jnp.pad(v, ((0, 0), (0, width - v.shape[1])))
    return v


def _pad_rows(w, rows):
    return jnp.pad(w.astype(F32), ((0, rows - w.shape[0]), (0, 0)))


def _ffn_weights(w_up, conv_w, w_down):
    d = w_up.shape[0]
    wu = w_up.astype(BF16).reshape(d, 2, N_FF_CHUNKS, FF_CHUNK).transpose(1, 2, 0, 3)
    cw = _pad_rows(conv_w, SUBLANES).reshape(SUBLANES, 2, N_FF_CHUNKS, FF_CHUNK).transpose(2, 1, 0, 3)
    wd = w_down.astype(BF16).reshape(N_FF_CHUNKS, FF_CHUNK, d)
    return wu[0], wu[1], cw, wd


def kernel(x, mix0_norm, w_in0, a_dw_w, a_dw_b, a_ln_w, a_ln_b, gdn_conv_w, gdn_a_log,
           gdn_dt_bias, gdn_onorm_w, w_out0, mix1_norm, w_qkv1, w_out1,
           ffn_norm, w_up, ffn_conv_w, w_down, final_norm):
    B, L, D = x.shape
    assert B == 1 and D == D_MODEL
    xs = x.reshape(L, D)

    w_in = w_in0[0].astype(BF16)
    w_main = w_in[:, :IN0_MAIN]
    w_ab = jnp.pad(w_in[:, IN0_MAIN:], ((0, 0), (0, LANES - (w_in.shape[1] - IN0_MAIN))))
    p, ab = _rms_matmul(xs, _row(mix0_norm[0]), w_main, jnp.ones((1, IN0_MAIN), F32), BF16,
                        tm=1024, tn=768, w_side=w_ab)
    y_a = _conformer(p, _pad_rows(a_dw_w[0], CONF_HALO), _row(a_dw_b[0]), _row(a_ln_w[0]),
                     _row(a_ln_b[0]), tm=512)
    y_b = _gdn(p, ab, _pad_rows(gdn_conv_w[0], SUBLANES), _row(gdn_a_log[0], LANES),
               _row(gdn_dt_bias[0], LANES), _row(gdn_onorm_w[0]), tb=2 * GDN_CHUNK)
    xs = _proj_residual(y_a, 0, y_b, 0, w_out0[0].astype(BF16), xs, tm=512)
    wg, wv, cw, wd = _ffn_weights(w_up[0], ffn_conv_w[0], w_down[0])
    xs = _ffn(xs, _row(ffn_norm[0]), wg, wv, cw, wd, _row(final_norm), tm=512, final_norm=False)

    hd = SB_HEADS * SB_HD
    qscale = jnp.concatenate([jnp.full((1, hd), SB_HD ** -0.5 * LOG2E, F32),
                              jnp.ones((1, 2 * hd), F32)], axis=1)
    qkv = _rms_matmul(xs, _row(mix1_norm[0]), w_qkv1[0].astype(BF16), qscale, BF16, tm=1024, tn=768)
    ki = lax.broadcasted_iota(jnp.int32, (ATT_TK, ATT_TK), 0)
    si = lax.broadcasted_iota(jnp.int32, (ATT_TK, ATT_TK), 1)
    umat = (ki >= si).astype(BF16)
    o = _attention(qkv, umat, tq=ATT_TQ, pairs_per_step=ATT_PAIRS)
    xs = _proj_residual(o, 0, o, 1, w_out1[0].astype(BF16), xs, tm=512)
    wg, wv, cw, wd = _ffn_weights(w_up[1], ffn_conv_w[1], w_down[1])
    xs = _ffn(xs, _row(ffn_norm[1]), wg, wv, cw, wd, _row(final_norm), tm=512, final_norm=True)
    return xs.reshape(B, L, D)
```

```python
import functools
import math

import jax
import jax.numpy as jnp
from jax import lax
from jax.experimental import pallas as pl
from jax.experimental.pallas import tpu as pltpu

F32 = jnp.float32
BF16 = jnp.bfloat16

D_MODEL = 1024
RMS_EPS = 1e-6
LN_EPS = 1e-5
CONV_CH = 512
CONV_WIDTH = 31
GDN_HEADS = 4
GDN_DK = 128
GDN_KEY = GDN_HEADS * GDN_DK
SHORT_CONV = 4
SB_HEADS = 16
SB_HD = 64
D_FF = 2816
FFN_CONV = 3

LANES = 128
SUBLANES = 8
VMEM_LIMIT = 56 * 1024 * 1024

IN0_MAIN = 3072
FF_CHUNK = 256
N_FF_CHUNKS = D_FF // FF_CHUNK
GDN_CHUNK = 128
CONF_HALO = 32
CONF_SUB = 32
ATT_TQ = 256
ATT_TK = 256
ATT_PAIRS = 2
LOG2E = 1.4426950408889634
ATT_STOP = 160.0


def _cparams(*sem):
    return pltpu.CompilerParams(dimension_semantics=sem, vmem_limit_bytes=VMEM_LIMIT)


def _bdot(a, b):
    return jnp.dot(a.astype(BF16), b.astype(BF16), preferred_element_type=F32)


def _bdot_nt(a, b):
    return lax.dot_general(a.astype(BF16), b.astype(BF16), (((1,), (1,)), ((), ())),
                           preferred_element_type=F32)


def _rms_matmul_kernel(*refs, has_side, tn):
    if has_side:
        x_ref, nw_ref, w_ref, cs_ref, ws_ref, o_ref, os_ref, h_ref = refs
    else:
        x_ref, nw_ref, w_ref, cs_ref, o_ref, h_ref = refs

    x = x_ref[...]
    ms = jnp.mean(x * x, axis=-1, keepdims=True)
    h_ref[...] = (x * lax.rsqrt(ms + RMS_EPS) * nw_ref[...]).astype(BF16)
    if has_side:
        os_ref[...] = jnp.dot(h_ref[...], ws_ref[...], preferred_element_type=F32)
    for n in range(w_ref.shape[1] // tn):
        cols = slice(n * tn, (n + 1) * tn)
        acc = jnp.dot(h_ref[...], w_ref[:, cols], preferred_element_type=F32)
        o_ref[:, cols] = (acc * cs_ref[:, cols]).astype(o_ref.dtype)


def _rms_matmul(x, nw, w, colscale, out_dtype, tm, tn, w_side=None):
    L, D = x.shape
    N = w.shape[1]
    has_side = w_side is not None
    const = lambda i: (0, 0)
    in_specs = [
        pl.BlockSpec((tm, D), lambda i: (i, 0)),
        pl.BlockSpec((1, D), const),
        pl.BlockSpec((D, N), const, pipeline_mode=pl.Buffered(1)),
        pl.BlockSpec((1, N), const),
    ]
    out_specs = [pl.BlockSpec((tm, N), lambda i: (i, 0))]
    out_shape = [jax.ShapeDtypeStruct((L, N), out_dtype)]
    args = [x, nw, w, colscale]
    if has_side:
        ns = w_side.shape[1]
        in_specs.append(pl.BlockSpec((D, ns), const))
        out_specs.append(pl.BlockSpec((tm, ns), lambda i: (i, 0)))
        out_shape.append(jax.ShapeDtypeStruct((L, ns), F32))
        args.append(w_side)
    outs = pl.pallas_call(
        functools.partial(_rms_matmul_kernel, has_side=has_side, tn=tn),
        grid=(L // tm,),
        in_specs=in_specs,
        out_specs=out_specs,
        out_shape=out_shape,
        scratch_shapes=[pltpu.VMEM((tm, D), BF16)],
        compiler_params=_cparams("arbitrary"),
        name="rms_matmul",
    )(*args)
    return outs if has_side else outs[0]


def _conformer_kernel(val_ref, gate_ref, w_ref, b_ref, lnw_ref, lnb_ref, o_ref, ext_ref, sh_ref):
    tm = val_ref.shape[0]
    i = pl.program_id(0)

    @pl.when(i == 0)
    def _():
        ext_ref[0:CONF_HALO, :] = jnp.zeros((CONF_HALO, CONV_CH), F32)

    @pl.when(i > 0)
    def _():
        ext_ref[0:CONF_HALO, :] = ext_ref[tm:tm + CONF_HALO, :]

    ext_ref[CONF_HALO:, :] = val_ref[...].astype(F32) * jax.nn.sigmoid(gate_ref[...].astype(F32))

    nrows = tm + CONF_HALO - SUBLANES
    for b in range(1, SUBLANES):
        sh_ref[b, 0:nrows, :] = ext_ref[pl.ds(b, nrows), :]

    first = CONF_HALO - (CONV_WIDTH - 1)

    def body(r, carry):
        base = pl.multiple_of(r * CONF_SUB, CONF_SUB)
        acc = jnp.zeros((CONF_SUB, CONV_CH), F32)
        for k in range(CONV_WIDTH):
            a, b = divmod(first + k, SUBLANES)
            src = ext_ref if b == 0 else sh_ref.at[b]
            acc = acc + src[pl.ds(base + SUBLANES * a, CONF_SUB), :] * w_ref[k:k + 1, :]
        acc = acc + b_ref[...]
        mu = jnp.mean(acc, axis=-1, keepdims=True)
        xc = acc - mu
        var = jnp.mean(xc * xc, axis=-1, keepdims=True)
        y = xc * lax.rsqrt(var + LN_EPS) * lnw_ref[...] + lnb_ref[...]
        o_ref[pl.ds(base, CONF_SUB), :] = (y * jax.nn.sigmoid(y)).astype(o_ref.dtype)
        return carry

    lax.fori_loop(0, tm // CONF_SUB, body, 0, unroll=2)


def _conformer(p, dw_w, dw_b, ln_w, ln_b, tm):
    L = p.shape[0]
    row = lambda i: (0, 0)
    return pl.pallas_call(
        _conformer_kernel,
        grid=(L // tm,),
        in_specs=[
            pl.BlockSpec((tm, CONV_CH), lambda i: (i, 0)),
            pl.BlockSpec((tm, CONV_CH), lambda i: (i, 1)),
            pl.BlockSpec((CONF_HALO, CONV_CH), row),
            pl.BlockSpec((1, CONV_CH), row),
            pl.BlockSpec((1, CONV_CH), row),
            pl.BlockSpec((1, CONV_CH), row),
        ],
        out_specs=pl.BlockSpec((tm, CONV_CH), lambda i: (i, 0)),
        out_shape=jax.ShapeDtypeStruct((L, CONV_CH), BF16),
        scratch_shapes=[pltpu.VMEM((tm + CONF_HALO, CONV_CH), F32),
                        pltpu.VMEM((SUBLANES, tm + CONF_HALO, CONV_CH), F32)],
        compiler_params=_cparams("arbitrary"),
        name="conformer_conv",
    )(p, p, dw_w, dw_b, ln_w, ln_b)


def _bmm(a, b):
    return lax.dot_general(a.astype(BF16), b.astype(BF16), (((2,), (1,)), ((0,), (0,))),
                           preferred_element_type=F32)


def _bmm_nt(a, b):
    return lax.dot_general(a.astype(BF16), b.astype(BF16), (((2,), (2,)), ((0,), (0,))),
                           preferred_element_type=F32)


def _unit_lower_inverse(low, row, col):
    n = low.shape[-1]
    eye = (row == col).astype(F32)

    def same_block(b):
        sh = b.bit_length() - 1
        return jnp.right_shift(row, sh) == jnp.right_shift(col, sh)

    base = SUBLANES
    a = jnp.where(same_block(base), low, 0.0)
    a2 = _bmm(a, a)
    a4 = _bmm(a2, a2)
    ia = eye - a
    p1 = ia + _bmm(ia, a2)
    t = p1 + _bmm(p1, a4)
    b = base
    while b < n:
        c = jnp.where(same_block(2 * b) & jnp.logical_not(same_block(b)), low, 0.0)
        t = t - _bmm(_bmm(t, c), t)
        b *= 2
    return t


def _gdn_kernel(q_ref, k_ref, v_ref, z_ref, ab_ref, cw_ref, alog_ref, dtb_ref, onw_ref,
                o_ref, ext_ref, s_ref, qkv_ref, gc_ref, gct_ref, beta_ref):
    tb = q_ref.shape[0]
    nc = tb // GDN_CHUNK
    c_ = GDN_CHUNK
    i = pl.program_id(0)

    @pl.when(i == 0)
    def _():
        ext_ref[0:SUBLANES, :] = jnp.zeros((SUBLANES, 3 * GDN_KEY), F32)
        s_ref[...] = jnp.zeros(s_ref.shape, F32)

    @pl.when(i > 0)
    def _():
        ext_ref[0:SUBLANES, :] = ext_ref[tb:tb + SUBLANES, :]

    ext_ref[SUBLANES:, 0:GDN_KEY] = q_ref[...].astype(F32)
    ext_ref[SUBLANES:, GDN_KEY:2 * GDN_KEY] = k_ref[...].astype(F32)
    ext_ref[SUBLANES:, 2 * GDN_KEY:] = v_ref[...].astype(F32)

    first = SUBLANES - (SHORT_CONV - 1)
    conv = ext_ref[pl.ds(first, tb), :] * cw_ref[0:1, :]
    for t in range(1, SHORT_CONV):
        conv = conv + ext_ref[pl.ds(first + t, tb), :] * cw_ref[t:t + 1, :]
    qkv = conv * jax.nn.sigmoid(conv)

    ab = ab_ref[...]
    g = -jnp.exp(alog_ref[...]) * jax.nn.softplus(ab + dtb_ref[...])
    beta = jax.nn.sigmoid(ab)

    rr = lax.broadcasted_iota(jnp.int32, (tb, tb), 0)
    cc = lax.broadcasted_iota(jnp.int32, (tb, tb), 1)
    csh = c_.bit_length() - 1
    tri = ((rr >= cc) & (jnp.right_shift(rr, csh) == jnp.right_shift(cc, csh))).astype(F32)
    gc = jnp.dot(tri, g, preferred_element_type=F32, precision=lax.Precision.HIGHEST)
    qkv_ref[...] = qkv
    gc_ref[...] = gc
    gct_ref[...] = gc.T
    beta_ref[...] = beta

    row = lax.broadcasted_iota(jnp.int32, (c_, c_), 0)
    col = lax.broadcasted_iota(jnp.int32, (c_, c_), 1)
    incl = row >= col
    strict = row > col

    pairs = [(c, h) for c in range(nc) for h in range(GDN_HEADS)]

    def stack(fn):
        return jnp.stack([fn(c * c_, h) for c, h in pairs])

    q = stack(lambda r0, h: qkv_ref[r0:r0 + c_, h * GDN_DK:(h + 1) * GDN_DK])
    k = stack(lambda r0, h: qkv_ref[r0:r0 + c_, GDN_KEY + h * GDN_DK:GDN_KEY + (h + 1) * GDN_DK])
    v = stack(lambda r0, h: qkv_ref[r0:r0 + c_, 2 * GDN_KEY + h * GDN_DK:2 * GDN_KEY + (h + 1) * GDN_DK])
    gcol = stack(lambda r0, h: gc_ref[r0:r0 + c_, h:h + 1])
    grow = stack(lambda r0, h: gct_ref[h:h + 1, r0:r0 + c_])
    bcol = stack(lambda r0, h: beta_ref[r0:r0 + c_, GDN_HEADS + h:GDN_HEADS + h + 1])
    glast = gcol[:, c_ - 1:c_, :]

    q = q * lax.rsqrt(jnp.sum(q * q, axis=-1, keepdims=True) + RMS_EPS) * (GDN_DK ** -0.5)
    k = k * lax.rsqrt(jnp.sum(k * k, axis=-1, keepdims=True) + RMS_EPS)
    decay = jnp.where(incl, jnp.exp(gcol - grow), 0.0)
    egc = jnp.exp(gcol)
    kb = k * bcol
    low = jnp.where(strict, _bmm_nt(kb, k) * decay, 0.0)
    tinv = _unit_lower_inverse(low, row, col)
    uw = _bmm(tinv, jnp.concatenate([v * bcol, kb * egc], axis=-1))
    qk = _bmm_nt(q, k) * decay
    ktt = jnp.swapaxes(k * jnp.exp(glast - gcol), 1, 2)
    lhs_state = jnp.concatenate([uw[:, :, GDN_DK:], q * egc], axis=1)
    lhs_vnew = jnp.concatenate([qk, ktt], axis=1)
    u = uw[:, :, :GDN_DK]
    sdecay = jnp.exp(glast)

    s = s_ref[...]
    for c in range(nc):
        sl = slice(c * GDN_HEADS, (c + 1) * GDN_HEADS)
        from_state = _bmm(lhs_state[sl], s)
        v_new = u[sl] - from_state[:, :c_]
        from_vnew = _bmm(lhs_vnew[sl], v_new)
        o = from_state[:, c_:] + from_vnew[:, :c_]
        s = s * sdecay[sl] + from_vnew[:, c_:]
        on = o * lax.rsqrt(jnp.mean(o * o, axis=-1, keepdims=True) + RMS_EPS) * onw_ref[...]
        r0 = c * c_
        for h in range(GDN_HEADS):
            zh = z_ref[r0:r0 + c_, h * GDN_DK:(h + 1) * GDN_DK].astype(F32)
            o_ref[r0:r0 + c_, h * GDN_DK:(h + 1) * GDN_DK] = (
                on[h] * (zh * jax.nn.sigmoid(zh))).astype(o_ref.dtype)
    s_ref[...] = s


def _gdn(p, ab, conv_w, alog_row, dtb_row, onorm_row, tb):
    L = p.shape[0]
    row = lambda i: (0, 0)
    return pl.pallas_call(
        _gdn_kernel,
        grid=(L // tb,),
        in_specs=[
            pl.BlockSpec((tb, GDN_KEY), lambda i: (i, 2)),
            pl.BlockSpec((tb, GDN_KEY), lambda i: (i, 3)),
            pl.BlockSpec((tb, GDN_KEY), lambda i: (i, 4)),
            pl.BlockSpec((tb, GDN_KEY), lambda i: (i, 5)),
            pl.BlockSpec((tb, LANES), lambda i: (i, 0)),
            pl.BlockSpec((SUBLANES, 3 * GDN_KEY), row),
            pl.BlockSpec((1, LANES), row),
            pl.BlockSpec((1, LANES), row),
            pl.BlockSpec((1, LANES), row),
        ],
        out_specs=pl.BlockSpec((tb, GDN_KEY), lambda i: (i, 0)),
        out_shape=jax.ShapeDtypeStruct((L, GDN_KEY), BF16),
        scratch_shapes=[pltpu.VMEM((tb + SUBLANES, 3 * GDN_KEY), F32),
                        pltpu.VMEM((GDN_HEADS, GDN_DK, GDN_DK), F32),
                        pltpu.VMEM((tb, 3 * GDN_KEY), F32),
                        pltpu.VMEM((tb, LANES), F32),
                        pltpu.VMEM((LANES, tb), F32),
                        pltpu.VMEM((tb, LANES), F32)],
        compiler_params=_cparams("arbitrary"),
        name="gated_deltanet",
    )(p, p, p, p, ab, conv_w, alog_row, dtb_row, onorm_row)


def _ffn_kernel(x_ref, ya_ref, yb_ref, wo_ref, nw_ref, wg_ref, wv_ref, cw_ref, wd_ref, fnw_ref,
                o_ref, h_ref, acc_ref, ubuf_a, ubuf_b, carry_ref, *, final_norm):
    tm = x_ref.shape[0]
    kh = ya_ref.shape[1]
    i = pl.program_id(0)

    @pl.when(i == 0)
    def _():
        carry_ref[...] = jnp.zeros(carry_ref.shape, F32)

    x = (x_ref[...] + jnp.dot(ya_ref[...], wo_ref[0:kh, :], preferred_element_type=F32)
         + jnp.dot(yb_ref[...], wo_ref[kh:, :], preferred_element_type=F32))
    ms = jnp.mean(x * x, axis=-1, keepdims=True)
    h_ref[...] = (x * lax.rsqrt(ms + RMS_EPS) * nw_ref[...]).astype(BF16)
    acc_ref[...] = jnp.zeros(acc_ref.shape, F32)

    def up(c, ubuf):
        h = h_ref[...]
        for gv, w_ref in enumerate((wg_ref, wv_ref)):
            u = jnp.dot(h, w_ref[c], preferred_element_type=F32)
            ubuf[gv, 0:SUBLANES, :] = carry_ref[c, gv]
            ubuf[gv, SUBLANES:, :] = u
            carry_ref[c, gv] = u[tm - SUBLANES:, :]

    def down(c, ubuf):
        cw = cw_ref[c]
        gate, val = [
            functools.reduce(jnp.add, [
                ubuf[gv, pl.ds(SUBLANES - (FFN_CONV - 1) + t, tm), :] * cw[gv, t:t + 1, :]
                for t in range(FFN_CONV)])
            for gv in range(2)]
        act = (gate * jax.nn.sigmoid(gate) * val).astype(BF16)
        acc_ref[...] += jnp.dot(act, wd_ref[c], preferred_element_type=F32)

    up(0, ubuf_a)

    def body(j, carry):
        c = 2 * j
        up(c + 1, ubuf_b)
        down(c, ubuf_a)
        up(c + 2, ubuf_a)
        down(c + 1, ubuf_b)
        return carry

    assert N_FF_CHUNKS % 2 == 1
    lax.fori_loop(0, N_FF_CHUNKS // 2, body, 0)
    down(N_FF_CHUNKS - 1, ubuf_a)
    y = x + acc_ref[...]
    if final_norm:
        ms2 = jnp.mean(y * y, axis=-1, keepdims=True)
        y = y * lax.rsqrt(ms2 + RMS_EPS) * fnw_ref[...]
    o_ref[...] = y


def _ffn(x, ya, ca, yb, cb, w_out, nw, wu, layer, cw, wd, fnw, tm, final_norm):
    L, D = x.shape
    kh = w_out.shape[0] // 2
    const2 = lambda i: (0, 0)
    resident = dict(pipeline_mode=pl.Buffered(1))
    return pl.pallas_call(
        functools.partial(_ffn_kernel, final_norm=final_norm),
        grid=(L // tm,),
        in_specs=[
            pl.BlockSpec((tm, D), lambda i: (i, 0)),
            pl.BlockSpec((tm, kh), lambda i: (i, ca)),
            pl.BlockSpec((tm, kh), lambda i: (i, cb)),
            pl.BlockSpec((2 * kh, D), const2, **resident),
            pl.BlockSpec((1, D), const2),
            pl.BlockSpec((None, N_FF_CHUNKS, D, FF_CHUNK), lambda i: (layer, 0, 0, 0), **resident),
            pl.BlockSpec((None, N_FF_CHUNKS, D, FF_CHUNK), lambda i: (layer, 1, 0, 0), **resident),
            pl.BlockSpec((N_FF_CHUNKS, 2, SUBLANES, FF_CHUNK), lambda i: (0, 0, 0, 0)),
            pl.BlockSpec((N_FF_CHUNKS, FF_CHUNK, D), lambda i: (0, 0, 0), **resident),
            pl.BlockSpec((1, D), const2),
        ],
        out_specs=pl.BlockSpec((tm, D), lambda i: (i, 0)),
        out_shape=jax.ShapeDtypeStruct((L, D), F32),
        scratch_shapes=[pltpu.VMEM((tm, D), BF16),
                        pltpu.VMEM((tm, D), F32),
                        pltpu.VMEM((2, tm + SUBLANES, FF_CHUNK), F32),
                        pltpu.VMEM((2, tm + SUBLANES, FF_CHUNK), F32),
                        pltpu.VMEM((N_FF_CHUNKS, 2, SUBLANES, FF_CHUNK), F32)],
        compiler_params=_cparams("arbitrary"),
        name="conv_ffn",
    )(x, ya, yb, w_out, nw, wu, wu, cw, wd, fnw)


def _chunk_weights_kernel(w_ref, o_ref):
    o_ref[0, 0] = w_ref[0].astype(o_ref.dtype)


def _chunk_up_weights(w_up):
    nl, d, n = w_up.shape
    nchunk = n // FF_CHUNK
    return pl.pallas_call(
        _chunk_weights_kernel,
        grid=(nl, nchunk),
        in_specs=[pl.BlockSpec((1, d, FF_CHUNK), lambda l, c: (l, 0, c))],
        out_specs=pl.BlockSpec((1, 1, d, FF_CHUNK), lambda l, c: (l, c, 0, 0)),
        out_shape=jax.ShapeDtypeStruct((nl, nchunk, d, FF_CHUNK), BF16),
        compiler_params=_cparams("arbitrary", "arbitrary"),
        name="chunk_up_weights",
    )(w_up)


def _softplus2(z):
    return jnp.where(z > 30.0, z, jnp.log2(1.0 + jnp.exp2(jnp.minimum(z, 30.0))))


def _attn_kernel(q_ref, k_ref, v_ref, u_ref, o_ref):
    tq = q_ref.shape[0]
    tk = u_ref.shape[0]
    qi = pl.program_id(1)
    n_diag = tq // tk
    n_pair = q_ref.shape[1] // LANES
    lane = lax.broadcasted_iota(jnp.int32, (1, LANES), 1)
    umat = u_ref[...]
    rows = 2 * tq
    rowq = jnp.bitwise_and(lax.broadcasted_iota(jnp.int32, (rows, tk), 0), tq - 1)
    colk = lax.broadcasted_iota(jnp.int32, (rows, tk), 1)

    qs = []
    for p in range(n_pair):
        q = q_ref[:, p * LANES:(p + 1) * LANES]
        zero = jnp.zeros_like(q)
        qs.append(jnp.concatenate([jnp.where(lane < SB_HD, q, zero),
                                   jnp.where(lane >= SB_HD, q, zero)], axis=0))

    def tile(kb, runs, accs, diag_offset):
        start = pl.multiple_of(kb * tk, tk)
        zs, sps = [], []
        for p in range(n_pair):
            k = k_ref[pl.ds(start, tk), p * LANES:(p + 1) * LANES]
            z = lax.dot_general(qs[p], k, (((1,), (1,)), ((), ())), preferred_element_type=F32)
            sp = _softplus2(z)
            if diag_offset is not None:
                valid = (colk + diag_offset) < rowq
                sp = jnp.where(valid, sp, 0.0)
            zs.append(z)
            sps.append(sp.astype(BF16))
        cum_all = jnp.dot(jnp.concatenate(sps, axis=0), umat, preferred_element_type=F32)
        new_runs, new_accs = [], []
        for p in range(n_pair):
            v = v_ref[pl.ds(start, tk), p * LANES:(p + 1) * LANES]
            cum = cum_all[p * rows:(p + 1) * rows]
            att = jnp.exp2(zs[p] - cum - runs[p])
            if diag_offset is not None:
                att = jnp.where(valid, att, 0.0)
            new_accs.append(accs[p] + jnp.dot(att.astype(BF16), v, preferred_element_type=F32))
            new_runs.append(runs[p] + cum[:, 0:1])
        return tuple(new_runs), tuple(new_accs)

    assert tq & (tq - 1) == 0
    runs = tuple(jnp.zeros((rows, 1), F32) for _ in range(n_pair))
    accs = tuple(jnp.zeros((rows, LANES), F32) for _ in range(n_pair))
    for d in range(n_diag):
        dd = n_diag - 1 - d
        runs, accs = tile(qi * n_diag + dd, runs, accs, dd * tk)

    def cond(state):
        kb, runs, _ = state
        least = functools.reduce(jnp.minimum, runs)
        return (kb >= 0) & (jnp.min(least) < ATT_STOP)

    def body(state):
        kb, runs, accs = state
        runs, accs = tile(kb, runs, accs, None)
        return kb - 1, runs, accs

    _, _, accs = lax.while_loop(cond, body, (qi * n_diag - 1, runs, accs))
    for p in range(n_pair):
        o_ref[:, p * LANES:(p + 1) * LANES] = jnp.where(
            lane < SB_HD, accs[p][:tq], accs[p][tq:]).astype(o_ref.dtype)


def _attention(qkv, umat, tq, pairs_per_step):
    L = qkv.shape[0]
    n_pairs = SB_HEADS * SB_HD // LANES
    n_steps = n_pairs // pairs_per_step
    width = pairs_per_step * LANES
    tk = umat.shape[0]
    return pl.pallas_call(
        _attn_kernel,
        grid=(n_steps, L // tq),
        in_specs=[
            pl.BlockSpec((tq, width), lambda hp, i: (i, hp)),
            pl.BlockSpec((L, width), lambda hp, i: (0, n_steps + hp)),
            pl.BlockSpec((L, width), lambda hp, i: (0, 2 * n_steps + hp)),
            pl.BlockSpec((tk, tk), lambda hp, i: (0, 0)),
        ],
        out_specs=pl.BlockSpec((tq, width), lambda hp, i: (i, hp)),
        out_shape=jax.ShapeDtypeStruct((L, SB_HEADS * SB_HD), BF16),
        compiler_params=_cparams("arbitrary", "arbitrary"),
        name="stickbreak_attention",
    )(qkv, qkv, qkv, umat)


def _row(v, width=None):
    v = v.astype(F32).reshape(1, -1)
    if width is not None and v.shape[1] < width:
        v = jnp.pad(v, ((0, 0), (0, width - v.shape[1])))
    return v


def _pad_rows(w, rows):
    return jnp.pad(w.astype(F32), ((0, rows - w.shape[0]), (0, 0)))


def _ffn_small_weights(conv_w, w_down):
    d = w_down.shape[1]
    cw = _pad_rows(conv_w, SUBLANES).reshape(SUBLANES, 2, N_FF_CHUNKS, FF_CHUNK).transpose(2, 1, 0, 3)
    wd = w_down.astype(BF16).reshape(N_FF_CHUNKS, FF_CHUNK, d)
    return cw, wd


def kernel(x, mix0_norm, w_in0, a_dw_w, a_dw_b, a_ln_w, a_ln_b, gdn_conv_w, gdn_a_log,
           gdn_dt_bias, gdn_onorm_w, w_out0, mix1_norm, w_qkv1, w_out1,
           ffn_norm, w_up, ffn_conv_w, w_down, final_norm):
    B, L, D = x.shape
    assert B == 1 and D == D_MODEL
    xs = x.reshape(L, D)

    w_in = w_in0[0].astype(BF16)
    w_main = w_in[:, :IN0_MAIN]
    w_ab = jnp.pad(w_in[:, IN0_MAIN:], ((0, 0), (0, LANES - (w_in.shape[1] - IN0_MAIN))))
    p, ab = _rms_matmul(xs, _row(mix0_norm[0]), w_main, jnp.ones((1, IN0_MAIN), F32), BF16,
                        tm=512, tn=768, w_side=w_ab)
    wu = _chunk_up_weights(w_up)
    y_a = _conformer(p, _pad_rows(a_dw_w[0], CONF_HALO), _row(a_dw_b[0]), _row(a_ln_w[0]),
                     _row(a_ln_b[0]), tm=512)
    y_b = _gdn(p, ab, _pad_rows(gdn_conv_w[0], SUBLANES), _row(gdn_a_log[0], LANES),
               _row(gdn_dt_bias[0], LANES), _row(gdn_onorm_w[0]), tb=2 * GDN_CHUNK)
    cw, wd = _ffn_small_weights(ffn_conv_w[0], w_down[0])
    xs = _ffn(xs, y_a, 0, y_b, 0, w_out0[0].astype(BF16), _row(ffn_norm[0]), wu, 0, cw, wd,
              _row(final_norm), tm=512, final_norm=False)

    hd = SB_HEADS * SB_HD
    qscale = jnp.concatenate([jnp.full((1, hd), SB_HD ** -0.5 * LOG2E, F32),
                              jnp.ones((1, 2 * hd), F32)], axis=1)
    qkv = _rms_matmul(xs, _row(mix1_norm[0]), w_qkv1[0].astype(BF16), qscale, BF16, tm=512, tn=768)
    ki = lax.broadcasted_iota(jnp.int32, (ATT_TK, ATT_TK), 0)
    si = lax.broadcasted_iota(jnp.int32, (ATT_TK, ATT_TK), 1)
    umat = (ki >= si).astype(BF16)
    o = _attention(qkv, umat, tq=ATT_TQ, pairs_per_step=ATT_PAIRS)
    cw, wd = _ffn_small_weights(ffn_conv_w[1], w_down[1])
    xs = _ffn(xs, o, 0, o, 1, w_out1[0].astype(BF16), _row(ffn_norm[1]), wu, 1, cw, wd,
              _row(final_norm), tm=512, final_norm=True)
    return xs.reshape(B, L, D)
```

```python
import functools
import math

import jax
import jax.numpy as jnp
from jax import lax
from jax.experimental import pallas as pl
from jax.experimental.pallas import tpu as pltpu

F32 = jnp.float32
BF16 = jnp.bfloat16

D_MODEL = 1024
RMS_EPS = 1e-6
LN_EPS = 1e-5
CONV_CH = 512
CONV_WIDTH = 31
GDN_HEADS = 4
GDN_DK = 128
GDN_KEY = GDN_HEADS * GDN_DK
SHORT_CONV = 4
SB_HEADS = 16
SB_HD = 64
D_FF = 2816
FFN_CONV = 3

LANES = 128
SUBLANES = 8
VMEM_LIMIT = 56 * 1024 * 1024

IN0_MAIN = 3072
FF_CHUNK = 256
N_FF_CHUNKS = D_FF // FF_CHUNK
FFN_TM = 512
GDN_CHUNK = 128
CONF_HALO = 32
CONF_SUB = 32
ATT_TQ = 256
ATT_TK = 256
ATT_PAIRS = 4
LOG2E = 1.4426950408889634
ATT_STOP = 160.0
ATT_OFF = -1e30


def _cparams(*sem):
    return pltpu.CompilerParams(dimension_semantics=sem, vmem_limit_bytes=VMEM_LIMIT)


def _bdot(a, b):
    return jnp.dot(a.astype(BF16), b.astype(BF16), preferred_element_type=F32)


def _bdot_nt(a, b):
    return lax.dot_general(a.astype(BF16), b.astype(BF16), (((1,), (1,)), ((), ())),
                           preferred_element_type=F32)


def _rms_matmul_kernel(*refs, has_side, tn):
    if has_side:
        x_ref, nw_ref, w_ref, cs_ref, ws_ref, o_ref, os_ref, h_ref = refs
    else:
        x_ref, nw_ref, w_ref, cs_ref, o_ref, h_ref = refs

    x = x_ref[...]
    ms = jnp.mean(x * x, axis=-1, keepdims=True)
    h_ref[...] = (x * lax.rsqrt(ms + RMS_EPS) * nw_ref[...]).astype(BF16)
    if has_side:
        os_ref[...] = jnp.dot(h_ref[...], ws_ref[...], preferred_element_type=F32)
    for n in range(w_ref.shape[1] // tn):
        cols = slice(n * tn, (n + 1) * tn)
        acc = jnp.dot(h_ref[...], w_ref[:, cols], preferred_element_type=F32)
        o_ref[:, cols] = (acc * cs_ref[:, cols]).astype(o_ref.dtype)


def _rms_matmul(x, nw, w, colscale, out_dtype, tm, tn, w_side=None):
    L, D = x.shape
    N = w.shape[1]
    has_side = w_side is not None
    const = lambda i: (0, 0)
    in_specs = [
        pl.BlockSpec((tm, D), lambda i: (i, 0)),
        pl.BlockSpec((1, D), const),
        pl.BlockSpec((D, N), const, pipeline_mode=pl.Buffered(1)),
        pl.BlockSpec((1, N), const),
    ]
    out_specs = [pl.BlockSpec((tm, N), lambda i: (i, 0))]
    out_shape = [jax.ShapeDtypeStruct((L, N), out_dtype)]
    args = [x, nw, w, colscale]
    if has_side:
        ns = w_side.shape[1]
        in_specs.append(pl.BlockSpec((D, ns), const))
        out_specs.append(pl.BlockSpec((tm, ns), lambda i: (i, 0)))
        out_shape.append(jax.ShapeDtypeStruct((L, ns), F32))
        args.append(w_side)
    outs = pl.pallas_call(
        functools.partial(_rms_matmul_kernel, has_side=has_side, tn=tn),
        grid=(L // tm,),
        in_specs=in_specs,
        out_specs=out_specs,
        out_shape=out_shape,
        scratch_shapes=[pltpu.VMEM((tm, D), BF16)],
        compiler_params=_cparams("arbitrary"),
        name="rms_matmul",
    )(*args)
    return outs if has_side else outs[0]


def _conformer_kernel(val_ref, gate_ref, w_ref, b_ref, lnw_ref, lnb_ref, o_ref, ext_ref, sh_ref):
    tm = val_ref.shape[0]
    i = pl.program_id(0)

    @pl.when(i == 0)
    def _():
        ext_ref[0:CONF_HALO, :] = jnp.zeros((CONF_HALO, CONV_CH), F32)

    @pl.when(i > 0)
    def _():
        ext_ref[0:CONF_HALO, :] = ext_ref[tm:tm + CONF_HALO, :]

    ext_ref[CONF_HALO:, :] = val_ref[...].astype(F32) * jax.nn.sigmoid(gate_ref[...].astype(F32))

    nrows = tm + CONF_HALO - SUBLANES
    for b in range(1, SUBLANES):
        sh_ref[b, 0:nrows, :] = ext_ref[pl.ds(b, nrows), :]

    first = CONF_HALO - (CONV_WIDTH - 1)

    def body(r, carry):
        base = pl.multiple_of(r * CONF_SUB, CONF_SUB)
        acc = jnp.zeros((CONF_SUB, CONV_CH), F32)
        for k in range(CONV_WIDTH):
            a, b = divmod(first + k, SUBLANES)
            src = ext_ref if b == 0 else sh_ref.at[b]
            acc = acc + src[pl.ds(base + SUBLANES * a, CONF_SUB), :] * w_ref[k:k + 1, :]
        acc = acc + b_ref[...]
        mu = jnp.mean(acc, axis=-1, keepdims=True)
        xc = acc - mu
        var = jnp.mean(xc * xc, axis=-1, keepdims=True)
        y = xc * lax.rsqrt(var + LN_EPS) * lnw_ref[...] + lnb_ref[...]
        o_ref[pl.ds(base, CONF_SUB), :] = (y * jax.nn.sigmoid(y)).astype(o_ref.dtype)
        return carry

    lax.fori_loop(0, tm // CONF_SUB, body, 0, unroll=4)


def _conformer(p, dw_w, dw_b, ln_w, ln_b, tm):
    L = p.shape[0]
    row = lambda i: (0, 0)
    return pl.pallas_call(
        _conformer_kernel,
        grid=(L // tm,),
        in_specs=[
            pl.BlockSpec((tm, CONV_CH), lambda i: (i, 0)),
            pl.BlockSpec((tm, CONV_CH), lambda i: (i, 1)),
            pl.BlockSpec((CONF_HALO, CONV_CH), row),
            pl.BlockSpec((1, CONV_CH), row),
            pl.BlockSpec((1, CONV_CH), row),
            pl.BlockSpec((1, CONV_CH), row),
        ],
        out_specs=pl.BlockSpec((tm, CONV_CH), lambda i: (i, 0)),
        out_shape=jax.ShapeDtypeStruct((L, CONV_CH), BF16),
        scratch_shapes=[pltpu.VMEM((tm + CONF_HALO, CONV_CH), F32),
                        pltpu.VMEM((SUBLANES, tm + CONF_HALO, CONV_CH), F32)],
        compiler_params=_cparams("arbitrary"),
        name="conformer_conv",
    )(p, p, dw_w, dw_b, ln_w, ln_b)


def _bmm(a, b):
    return lax.dot_general(a.astype(BF16), b.astype(BF16), (((2,), (1,)), ((0,), (0,))),
                           preferred_element_type=F32)


def _bmm_nt(a, b):
    return lax.dot_general(a.astype(BF16), b.astype(BF16), (((2,), (2,)), ((0,), (0,))),
                           preferred_element_type=F32)


def _unit_lower_inverse(low, row, col):
    n = low.shape[-1]
    eye = (row == col).astype(F32)

    def same_block(b):
        sh = b.bit_length() - 1
        return jnp.right_shift(row, sh) == jnp.right_shift(col, sh)

    base = SUBLANES
    a = jnp.where(same_block(base), low, 0.0)
    a2 = _bmm(a, a)
    a4 = _bmm(a2, a2)
    ia = eye - a
    p1 = ia + _bmm(ia, a2)
    t = p1 + _bmm(p1, a4)
    b = base
    while b < n:
        c = jnp.where(same_block(2 * b) & jnp.logical_not(same_block(b)), low, 0.0)
        t = t - _bmm(_bmm(t, c), t)
        b *= 2
    return t


def _gdn_kernel(q_ref, k_ref, v_ref, z_ref, ab_ref, cw_ref, alog_ref, dtb_ref, onw_ref,
                o_ref, ext_ref, s_ref, qkv_ref, gc_ref, gct_ref, beta_ref):
    tb = q_ref.shape[0]
    nc = tb // GDN_CHUNK
    c_ = GDN_CHUNK
    i = pl.program_id(0)

    @pl.when(i == 0)
    def _():
        ext_ref[0:SUBLANES, :] = jnp.zeros((SUBLANES, 3 * GDN_KEY), F32)
        s_ref[...] = jnp.zeros(s_ref.shape, F32)

    @pl.when(i > 0)
    def _():
        ext_ref[0:SUBLANES, :] = ext_ref[tb:tb + SUBLANES, :]

    ext_ref[SUBLANES:, 0:GDN_KEY] = q_ref[...].astype(F32)
    ext_ref[SUBLANES:, GDN_KEY:2 * GDN_KEY] = k_ref[...].astype(F32)
    ext_ref[SUBLANES:, 2 * GDN_KEY:] = v_ref[...].astype(F32)

    first = SUBLANES - (SHORT_CONV - 1)
    conv = ext_ref[pl.ds(first, tb), :] * cw_ref[0:1, :]
    for t in range(1, SHORT_CONV):
        conv = conv + ext_ref[pl.ds(first + t, tb), :] * cw_ref[t:t + 1, :]
    qkv = conv * jax.nn.sigmoid(conv)

    ab = ab_ref[...]
    g = -jnp.exp(alog_ref[...]) * jax.nn.softplus(ab + dtb_ref[...])
    beta = jax.nn.sigmoid(ab)

    rr = lax.broadcasted_iota(jnp.int32, (tb, tb), 0)
    cc = lax.broadcasted_iota(jnp.int32, (tb, tb), 1)
    csh = c_.bit_length() - 1
    tri = ((rr >= cc) & (jnp.right_shift(rr, csh) == jnp.right_shift(cc, csh))).astype(F32)
    gc = jnp.dot(tri, g, preferred_element_type=F32, precision=lax.Precision.HIGHEST)
    qkv_ref[...] = qkv
    gc_ref[...] = gc
    gct_ref[...] = gc.T
    beta_ref[...] = beta

    row = lax.broadcasted_iota(jnp.int32, (c_, c_), 0)
    col = lax.broadcasted_iota(jnp.int32, (c_, c_), 1)
    incl = row >= col
    strict = row > col

    pairs = [(c, h) for c in range(nc) for h in range(GDN_HEADS)]

    def stack(fn):
        return jnp.stack([fn(c * c_, h) for c, h in pairs])

    q = stack(lambda r0, h: qkv_ref[r0:r0 + c_, h * GDN_DK:(h + 1) * GDN_DK])
    k = stack(lambda r0, h: qkv_ref[r0:r0 + c_, GDN_KEY + h * GDN_DK:GDN_KEY + (h + 1) * GDN_DK])
    v = stack(lambda r0, h: qkv_ref[r0:r0 + c_, 2 * GDN_KEY + h * GDN_DK:2 * GDN_KEY + (h + 1) * GDN_DK])
    gcol = stack(lambda r0, h: gc_ref[r0:r0 + c_, h:h + 1])
    grow = stack(lambda r0, h: gct_ref[h:h + 1, r0:r0 + c_])
    bcol = stack(lambda r0, h: beta_ref[r0:r0 + c_, GDN_HEADS + h:GDN_HEADS + h + 1])
    glast = gcol[:, c_ - 1:c_, :]

    q = q * lax.rsqrt(jnp.sum(q * q, axis=-1, keepdims=True) + RMS_EPS) * (GDN_DK ** -0.5)
    k = k * lax.rsqrt(jnp.sum(k * k, axis=-1, keepdims=True) + RMS_EPS)
    decay = jnp.where(incl, jnp.exp(gcol - grow), 0.0)
    egc = jnp.exp(gcol)
    kb = k * bcol
    low = jnp.where(strict, _bmm_nt(kb, k) * decay, 0.0)
    tinv = _unit_lower_inverse(low, row, col)
    uw = _bmm(tinv, jnp.concatenate([v * bcol, kb * egc], axis=-1))
    qk = _bmm_nt(q, k) * decay
    ktt = jnp.swapaxes(k * jnp.exp(glast - gcol), 1, 2)
    lhs_state = jnp.concatenate([uw[:, :, GDN_DK:], q * egc], axis=1)
    lhs_vnew = jnp.concatenate([qk, ktt], axis=1)
    u = uw[:, :, :GDN_DK]
    sdecay = jnp.exp(glast)

    s = s_ref[...]
    for c in range(nc):
        sl = slice(c * GDN_HEADS, (c + 1) * GDN_HEADS)
        from_state = _bmm(lhs_state[sl], s)
        v_new = u[sl] - from_state[:, :c_]
        from_vnew = _bmm(lhs_vnew[sl], v_new)
        o = from_state[:, c_:] + from_vnew[:, :c_]
        s = s * sdecay[sl] + from_vnew[:, c_:]
        on = o * lax.rsqrt(jnp.mean(o * o, axis=-1, keepdims=True) + RMS_EPS) * onw_ref[...]
        r0 = c * c_
        for h in range(GDN_HEADS):
            zh = z_ref[r0:r0 + c_, h * GDN_DK:(h + 1) * GDN_DK].astype(F32)
            o_ref[r0:r0 + c_, h * GDN_DK:(h + 1) * GDN_DK] = (
                on[h] * (zh * jax.nn.sigmoid(zh))).astype(o_ref.dtype)
    s_ref[...] = s


def _gdn(p, ab, conv_w, alog_row, dtb_row, onorm_row, tb):
    L = p.shape[0]
    row = lambda i: (0, 0)
    return pl.pallas_call(
        _gdn_kernel,
        grid=(L // tb,),
        in_specs=[
            pl.BlockSpec((tb, GDN_KEY), lambda i: (i, 2)),
            pl.BlockSpec((tb, GDN_KEY), lambda i: (i, 3)),
            pl.BlockSpec((tb, GDN_KEY), lambda i: (i, 4)),
            pl.BlockSpec((tb, GDN_KEY), lambda i: (i, 5)),
            pl.BlockSpec((tb, LANES), lambda i: (i, 0)),
            pl.BlockSpec((SUBLANES, 3 * GDN_KEY), row),
            pl.BlockSpec((1, LANES), row),
            pl.BlockSpec((1, LANES), row),
            pl.BlockSpec((1, LANES), row),
        ],
        out_specs=pl.BlockSpec((tb, GDN_KEY), lambda i: (i, 0)),
        out_shape=jax.ShapeDtypeStruct((L, GDN_KEY), BF16),
        scratch_shapes=[pltpu.VMEM((tb + SUBLANES, 3 * GDN_KEY), F32),
                        pltpu.VMEM((GDN_HEADS, GDN_DK, GDN_DK), F32),
                        pltpu.VMEM((tb, 3 * GDN_KEY), F32),
                        pltpu.VMEM((tb, LANES), F32),
                        pltpu.VMEM((LANES, tb), F32),
                        pltpu.VMEM((tb, LANES), F32)],
        compiler_params=_cparams("arbitrary"),
        name="gated_deltanet",
    )(p, p, p, p, ab, conv_w, alog_row, dtb_row, onorm_row)


def _ffn_kernel(x_ref, ya_ref, yb_ref, wo_ref, nw_ref, wg_ref, wv_ref, cw_ref, wd_ref, fnw_ref,
                o_ref, x1_ref, hf_ref, h_ref, acc_ref, ubuf_a, ubuf_b, carry_ref, *, final_norm):
    tm = x_ref.shape[0]
    kh = ya_ref.shape[1]
    nv = tm // SUBLANES
    pitch = nv + SUBLANES
    halo = (FFN_CONV - 1) * SUBLANES
    i = pl.program_id(0)

    @pl.when(i == 0)
    def _():
        carry_ref[...] = jnp.zeros(carry_ref.shape, F32)

    x = (x_ref[...] + jnp.dot(ya_ref[...], wo_ref[0:kh, :], preferred_element_type=F32)
         + jnp.dot(yb_ref[...], wo_ref[kh:, :], preferred_element_type=F32))
    x1_ref[...] = x
    ms = jnp.mean(x * x, axis=-1, keepdims=True)
    hn = x * lax.rsqrt(ms + RMS_EPS) * nw_ref[...]
    n_slab = hf_ref.shape[0]
    for s in range(SUBLANES):
        for l in range(n_slab):
            hf_ref[l, pitch * s:pitch * s + nv, :] = hn[nv * s:nv * (s + 1), l * LANES:(l + 1) * LANES]
    for v in range(0, nv, 2):
        for l in range(n_slab):
            pair = jnp.concatenate([hf_ref[l, pl.ds(v, SUBLANES, stride=pitch), :],
                                    hf_ref[l, pl.ds(v + 1, SUBLANES, stride=pitch), :]], axis=0)
            h_ref[SUBLANES * v:SUBLANES * (v + 2), l * LANES:(l + 1) * LANES] = pair.astype(BF16)
    acc_ref[...] = jnp.zeros(acc_ref.shape, F32)
    sub = lax.broadcasted_iota(jnp.int32, (SUBLANES, FF_CHUNK), 0)

    def up(c, ubuf):
        h = h_ref[...]
        for gv, w_ref in enumerate((wg_ref, wv_ref)):
            u = jnp.dot(h, w_ref[c], preferred_element_type=F32)
            prev = carry_ref[c, gv]
            for g in range(FFN_CONV - 1):
                lo = tm - halo + SUBLANES * g
                wrapped = pltpu.roll(u[lo:lo + SUBLANES, :], 1, 0)
                top = prev[SUBLANES * g + SUBLANES - 1:SUBLANES * (g + 1), :]
                ubuf[gv, SUBLANES * g:SUBLANES * (g + 1), :] = jnp.where(sub == 0, top, wrapped)
            ubuf[gv, halo:, :] = u
            carry_ref[c, gv] = u[tm - halo:, :]

    def down(c, ubuf):
        cw = cw_ref[c]
        gate, val = [
            functools.reduce(jnp.add, [
                ubuf[gv, pl.ds(SUBLANES * t, tm), :] * cw[gv, t:t + 1, :]
                for t in range(FFN_CONV)])
            for gv in range(2)]
        act = (gate * jax.nn.sigmoid(gate) * val).astype(BF16)
        d = jnp.dot(act, wd_ref[c], preferred_element_type=F32)
        for l in range(n_slab):
            acc_ref[l] += d[:, l * LANES:(l + 1) * LANES]

    ubufs = (ubuf_a, ubuf_b)
    up(0, ubufs[0])
    for c in range(N_FF_CHUNKS):
        if c + 1 < N_FF_CHUNKS:
            up(c + 1, ubufs[(c + 1) % 2])
        down(c, ubufs[c % 2])
    runs = nv // SUBLANES
    for j in range(nv):
        first = SUBLANES * SUBLANES * (j % runs) + j // runs
        rows = slice(SUBLANES * j, SUBLANES * (j + 1))
        y = x1_ref[rows, :] + jnp.concatenate(
            [acc_ref[l, pl.ds(first, SUBLANES, stride=SUBLANES), :] for l in range(n_slab)], axis=1)
        if final_norm:
            ms2 = jnp.mean(y * y, axis=-1, keepdims=True)
            y = y * lax.rsqrt(ms2 + RMS_EPS) * fnw_ref[...]
        o_ref[rows, :] = y


def _ffn(x, ya, ca, yb, cb, w_out, nw, wu, layer, cw, wd, fnw, tm, final_norm):
    L, D = x.shape
    kh = w_out.shape[0] // 2
    halo = (FFN_CONV - 1) * SUBLANES
    assert tm % (SUBLANES * SUBLANES) == 0
    const2 = lambda i: (0, 0)
    resident = dict(pipeline_mode=pl.Buffered(1))
    return pl.pallas_call(
        functools.partial(_ffn_kernel, final_norm=final_norm),
        grid=(L // tm,),
        in_specs=[
            pl.BlockSpec((tm, D), lambda i: (i, 0)),
            pl.BlockSpec((tm, kh), lambda i: (i, ca)),
            pl.BlockSpec((tm, kh), lambda i: (i, cb)),
            pl.BlockSpec((2 * kh, D), const2, **resident),
            pl.BlockSpec((1, D), const2),
            pl.BlockSpec((None, N_FF_CHUNKS, D, FF_CHUNK), lambda i: (layer, 0, 0, 0), **resident),
            pl.BlockSpec((None, N_FF_CHUNKS, D, FF_CHUNK), lambda i: (layer, 1, 0, 0), **resident),
            pl.BlockSpec((N_FF_CHUNKS, 2, SUBLANES, FF_CHUNK), lambda i: (0, 0, 0, 0)),
            pl.BlockSpec((N_FF_CHUNKS, FF_CHUNK, D), lambda i: (0, 0, 0), **resident),
            pl.BlockSpec((1, D), const2),
        ],
        out_specs=pl.BlockSpec((tm, D), lambda i: (i, 0)),
        out_shape=jax.ShapeDtypeStruct((L, D), F32),
        scratch_shapes=[pltpu.VMEM((tm, D), F32),
                        pltpu.VMEM((D // LANES, tm + SUBLANES * SUBLANES, LANES), F32),
                        pltpu.VMEM((tm, D), BF16),
                        pltpu.VMEM((D // LANES, tm, LANES), F32),
                        pltpu.VMEM((2, tm + halo, FF_CHUNK), F32),
                        pltpu.VMEM((2, tm + halo, FF_CHUNK), F32),
                        pltpu.VMEM((N_FF_CHUNKS, 2, halo, FF_CHUNK), F32)],
        compiler_params=_cparams("arbitrary"),
        name="conv_ffn",
    )(x, ya, yb, w_out, nw, wu, wu, cw, wd, fnw)


def _chunk_weights_kernel(w_ref, o_ref):
    for g in range(o_ref.shape[1]):
        o_ref[0, g] = w_ref[0, :, g * FF_CHUNK:(g + 1) * FF_CHUNK].astype(o_ref.dtype)


def _chunk_up_weights(w_up, per_step=2):
    nl, d, n = w_up.shape
    nchunk = n // FF_CHUNK
    return pl.pallas_call(
        _chunk_weights_kernel,
        grid=(nl, nchunk // per_step),
        in_specs=[pl.BlockSpec((1, d, per_step * FF_CHUNK), lambda l, c: (l, 0, c))],
        out_specs=pl.BlockSpec((1, per_step, d, FF_CHUNK), lambda l, c: (l, c, 0, 0)),
        out_shape=jax.ShapeDtypeStruct((nl, nchunk, d, FF_CHUNK), BF16),
        compiler_params=_cparams("arbitrary", "arbitrary"),
        name="chunk_up_weights",
    )(w_up)


def _softplus2(z):
    return jnp.where(z > 30.0, z, jnp.log2(1.0 + jnp.exp2(jnp.minimum(z, 30.0))))


def _attn_kernel(q_ref, k_ref, v_ref, u_ref, o_ref):
    tq = q_ref.shape[0]
    tk = u_ref.shape[0]
    qi = pl.program_id(1)
    n_diag = tq // tk
    n_pair = q_ref.shape[1] // LANES
    lane = lax.broadcasted_iota(jnp.int32, (1, LANES), 1)
    umat = u_ref[...]
    rows = 2 * tq
    rowq = jnp.bitwise_and(lax.broadcasted_iota(jnp.int32, (rows, tk), 0), tq - 1)
    colk = lax.broadcasted_iota(jnp.int32, (rows, tk), 1)

    qs = []
    for p in range(n_pair):
        q = q_ref[:, p * LANES:(p + 1) * LANES]
        zero = jnp.zeros_like(q)
        qs.append(jnp.concatenate([jnp.where(lane < SB_HD, q, zero),
                                   jnp.where(lane >= SB_HD, q, zero)], axis=0))

    def tile(kb, runs, accs, diag_offset, bias=None):
        start = pl.multiple_of(kb * tk, tk)
        zs, sps = [], []
        for p in range(n_pair):
            k = k_ref[pl.ds(start, tk), p * LANES:(p + 1) * LANES]
            z = lax.dot_general(qs[p], k, (((1,), (1,)), ((), ())), preferred_element_type=F32)
            if bias is not None:
                z = z + bias
            sp = _softplus2(z)
            if diag_offset is not None:
                valid = (colk + diag_offset) < rowq
                sp = jnp.where(valid, sp, 0.0)
            zs.append(z)
            sps.append(sp.astype(BF16))
        cum_all = jnp.dot(jnp.concatenate(sps, axis=0), umat, preferred_element_type=F32)
        new_runs, new_accs = [], []
        for p in range(n_pair):
            v = v_ref[pl.ds(start, tk), p * LANES:(p + 1) * LANES]
            cum = cum_all[p * rows:(p + 1) * rows]
            att = jnp.exp2(zs[p] - cum - runs[p])
            if diag_offset is not None:
                att = jnp.where(valid, att, 0.0)
            new_accs.append(accs[p] + jnp.dot(att.astype(BF16), v, preferred_element_type=F32))
            new_runs.append(runs[p] + cum[:, 0:1])
        return tuple(new_runs), tuple(new_accs)

    assert tq & (tq - 1) == 0
    runs = tuple(jnp.zeros((rows, 1), F32) for _ in range(n_pair))
    accs = tuple(jnp.zeros((rows, LANES), F32) for _ in range(n_pair))
    for d in range(n_diag):
        dd = n_diag - 1 - d
        runs, accs = tile(qi * n_diag + dd, runs, accs, dd * tk)
    has_left = qi * n_diag >= 1
    runs, accs = tile(jnp.maximum(qi * n_diag - 1, 0), runs, accs, None,
                      bias=jnp.where(has_left, 0.0, ATT_OFF).astype(F32))
    kb0 = qi * n_diag - 2

    def cond(state):
        kb, runs, _ = state
        least = functools.reduce(jnp.minimum, runs)
        return (kb >= 0) & (jnp.min(least) < ATT_STOP)

    def body(state):
        kb, runs, accs = state
        runs, accs = tile(kb, runs, accs, None)
        return kb - 1, runs, accs

    _, _, accs = lax.while_loop(cond, body, (kb0, runs, accs))
    for p in range(n_pair):
        o_ref[:, p * LANES:(p + 1) * LANES] = jnp.where(
            lane < SB_HD, accs[p][:tq], accs[p][tq:]).astype(o_ref.dtype)


def _attention(qkv, umat, tq, pairs_per_step):
    L = qkv.shape[0]
    n_pairs = SB_HEADS * SB_HD // LANES
    n_steps = n_pairs // pairs_per_step
    width = pairs_per_step * LANES
    tk = umat.shape[0]
    return pl.pallas_call(
        _attn_kernel,
        grid=(n_steps, L // tq),
        in_specs=[
            pl.BlockSpec((tq, width), lambda hp, i: (i, hp)),
            pl.BlockSpec((L, width), lambda hp, i: (0, n_steps + hp), pipeline_mode=pl.Buffered(1)),
            pl.BlockSpec((L, width), lambda hp, i: (0, 2 * n_steps + hp), pipeline_mode=pl.Buffered(1)),
            pl.BlockSpec((tk, tk), lambda hp, i: (0, 0)),
        ],
        out_specs=pl.BlockSpec((tq, width), lambda hp, i: (i, hp)),
        out_shape=jax.ShapeDtypeStruct((L, SB_HEADS * SB_HD), BF16),
        compiler_params=_cparams("arbitrary", "arbitrary"),
        name="stickbreak_attention",
    )(qkv, qkv, qkv, umat)


def _row(v, width=None):
    v = v.astype(F32).reshape(1, -1)
    if width is not None and v.shape[1] < width:
        v = jnp.pad(v, ((0, 0), (0, width - v.shape[1])))
    return v


def _pad_rows(w, rows):
    return jnp.pad(w.astype(F32), ((0, rows - w.shape[0]), (0, 0)))


def _ffn_small_weights(conv_w, w_down):
    d = w_down.shape[1]
    cw = _pad_rows(conv_w, SUBLANES).reshape(SUBLANES, 2, N_FF_CHUNKS, FF_CHUNK).transpose(2, 1, 0, 3)
    wd = w_down.astype(BF16).reshape(N_FF_CHUNKS, FF_CHUNK, d)
    return cw, wd


def kernel(x, mix0_norm, w_in0, a_dw_w, a_dw_b, a_ln_w, a_ln_b, gdn_conv_w, gdn_a_log,
           gdn_dt_bias, gdn_onorm_w, w_out0, mix1_norm, w_qkv1, w_out1,
           ffn_norm, w_up, ffn_conv_w, w_down, final_norm):
    B, L, D = x.shape
    assert B == 1 and D == D_MODEL
    xs = x.reshape(L, D)

    w_in = w_in0[0].astype(BF16)
    w_main = w_in[:, :IN0_MAIN]
    w_ab = jnp.pad(w_in[:, IN0_MAIN:], ((0, 0), (0, LANES - (w_in.shape[1] - IN0_MAIN))))
    p, ab = _rms_matmul(xs, _row(mix0_norm[0]), w_main, jnp.ones((1, IN0_MAIN), F32), BF16,
                        tm=512, tn=768, w_side=w_ab)
    wu = _chunk_up_weights(w_up)
    y_a = _conformer(p, _pad_rows(a_dw_w[0], CONF_HALO), _row(a_dw_b[0]), _row(a_ln_w[0]),
                     _row(a_ln_b[0]), tm=512)
    y_b = _gdn(p, ab, _pad_rows(gdn_conv_w[0], SUBLANES), _row(gdn_a_log[0], LANES),
               _row(gdn_dt_bias[0], LANES), _row(gdn_onorm_w[0]), tb=4 * GDN_CHUNK)
    cw, wd = _ffn_small_weights(ffn_conv_w[0], w_down[0])
    xs = _ffn(xs, y_a, 0, y_b, 0, w_out0[0].astype(BF16), _row(ffn_norm[0]), wu, 0, cw, wd,
              _row(final_norm), tm=FFN_TM, final_norm=False)

    hd = SB_HEADS * SB_HD
    qscale = jnp.concatenate([jnp.full((1, hd), SB_HD ** -0.5 * LOG2E, F32),
                              jnp.ones((1, 2 * hd), F32)], axis=1)
    qkv = _rms_matmul(xs, _row(mix1_norm[0]), w_qkv1[0].astype(BF16), qscale, BF16, tm=512, tn=768)
    ki = lax.broadcasted_iota(jnp.int32, (ATT_TK, ATT_TK), 0)
    si = lax.broadcasted_iota(jnp.int32, (ATT_TK, ATT_TK), 1)
    umat = (ki >= si).astype(BF16)
    o = _attention(qkv, umat, tq=ATT_TQ, pairs_per_step=ATT_PAIRS)
    cw, wd = _ffn_small_weights(ffn_conv_w[1], w_down[1])
    xs = _ffn(xs, o, 0, o, 1, w_out1[0].astype(BF16), _row(ffn_norm[1]), wu, 1, cw, wd,
              _row(final_norm), tm=FFN_TM, final_norm=True)
    return xs.reshape(B, L, D)
```

```python
import functools
import math

import jax
import jax.numpy as jnp
from jax import lax
from jax.experimental import pallas as pl
from jax.experimental.pallas import tpu as pltpu

F32 = jnp.float32
BF16 = jnp.bfloat16

D_MODEL = 1024
RMS_EPS = 1e-6
LN_EPS = 1e-5
CONV_CH = 512
CONV_WIDTH = 31
GDN_HEADS = 4
GDN_DK = 128
GDN_KEY = GDN_HEADS * GDN_DK
SHORT_CONV = 4
SB_HEADS = 16
SB_HD = 64
D_FF = 2816
FFN_CONV = 3

LANES = 128
SUBLANES = 8
VMEM_LIMIT = 60 * 1024 * 1024

IN0_MAIN = 3072
FF_CHUNK = 256
N_FF_CHUNKS = D_FF // FF_CHUNK
FFN_TM = 1024
PROJ_TM = 1024
GDN_CHUNK = 128
CONF_HALO = 32
CONF_SUB = 32
ATT_TQ = 256
ATT_TK = 256
ATT_PAIRS = 4
LOG2E = 1.4426950408889634
ATT_STOP = 160.0
ATT_OFF = -1e30


def _cparams(*sem):
    return pltpu.CompilerParams(dimension_semantics=sem, vmem_limit_bytes=VMEM_LIMIT)


def _bdot(a, b):
    return jnp.dot(a.astype(BF16), b.astype(BF16), preferred_element_type=F32)


def _bdot_nt(a, b):
    return lax.dot_general(a.astype(BF16), b.astype(BF16), (((1,), (1,)), ((), ())),
                           preferred_element_type=F32)


def _rms_matmul_kernel(*refs, has_side, tn):
    if has_side:
        x_ref, nw_ref, w_ref, cs_ref, ws_ref, o_ref, os_ref, h_ref = refs
    else:
        x_ref, nw_ref, w_ref, cs_ref, o_ref, h_ref = refs

    x = x_ref[...]
    ms = jnp.mean(x * x, axis=-1, keepdims=True)
    h_ref[...] = (x * lax.rsqrt(ms + RMS_EPS) * nw_ref[...]).astype(BF16)
    if has_side:
        os_ref[...] = jnp.dot(h_ref[...], ws_ref[...], preferred_element_type=F32)
    for n in range(w_ref.shape[1] // tn):
        cols = slice(n * tn, (n + 1) * tn)
        acc = jnp.dot(h_ref[...], w_ref[:, cols], preferred_element_type=F32)
        o_ref[:, cols] = (acc * cs_ref[:, cols]).astype(o_ref.dtype)


def _rms_matmul(x, nw, w, colscale, out_dtype, tm, tn, w_side=None):
    L, D = x.shape
    N = w.shape[1]
    has_side = w_side is not None
    const = lambda i: (0, 0)
    in_specs = [
        pl.BlockSpec((tm, D), lambda i: (i, 0)),
        pl.BlockSpec((1, D), const),
        pl.BlockSpec((D, N), const, pipeline_mode=pl.Buffered(1)),
        pl.BlockSpec((1, N), const),
    ]
    out_specs = [pl.BlockSpec((tm, N), lambda i: (i, 0))]
    out_shape = [jax.ShapeDtypeStruct((L, N), out_dtype)]
    args = [x, nw, w, colscale]
    if has_side:
        ns = w_side.shape[1]
        in_specs.append(pl.BlockSpec((D, ns), const))
        out_specs.append(pl.BlockSpec((tm, ns), lambda i: (i, 0)))
        out_shape.append(jax.ShapeDtypeStruct((L, ns), F32))
        args.append(w_side)
    outs = pl.pallas_call(
        functools.partial(_rms_matmul_kernel, has_side=has_side, tn=tn),
        grid=(L // tm,),
        in_specs=in_specs,
        out_specs=out_specs,
        out_shape=out_shape,
        scratch_shapes=[pltpu.VMEM((tm, D), BF16)],
        compiler_params=_cparams("arbitrary"),
        name="rms_matmul",
    )(*args)
    return outs if has_side else outs[0]


def _conformer_kernel(val_ref, gate_ref, w_ref, b_ref, lnw_ref, lnb_ref, o_ref, ext_ref, sh_ref):
    tm = val_ref.shape[0]
    i = pl.program_id(0)

    @pl.when(i == 0)
    def _():
        ext_ref[0:CONF_HALO, :] = jnp.zeros((CONF_HALO, CONV_CH), F32)

    @pl.when(i > 0)
    def _():
        ext_ref[0:CONF_HALO, :] = ext_ref[tm:tm + CONF_HALO, :]

    ext_ref[CONF_HALO:, :] = val_ref[...].astype(F32) * jax.nn.sigmoid(gate_ref[...].astype(F32))

    nrows = tm + CONF_HALO - SUBLANES
    for b in range(1, SUBLANES):
        sh_ref[b, 0:nrows, :] = ext_ref[pl.ds(b, nrows), :]

    first = CONF_HALO - (CONV_WIDTH - 1)

    def body(r, carry):
        base = pl.multiple_of(r * CONF_SUB, CONF_SUB)
        acc = jnp.zeros((CONF_SUB, CONV_CH), F32)
        for k in range(CONV_WIDTH):
            a, b = divmod(first + k, SUBLANES)
            src = ext_ref if b == 0 else sh_ref.at[b]
            acc = acc + src[pl.ds(base + SUBLANES * a, CONF_SUB), :] * w_ref[k:k + 1, :]
        acc = acc + b_ref[...]
        mu = jnp.mean(acc, axis=-1, keepdims=True)
        xc = acc - mu
        var = jnp.mean(xc * xc, axis=-1, keepdims=True)
        y = xc * lax.rsqrt(var + LN_EPS) * lnw_ref[...] + lnb_ref[...]
        o_ref[pl.ds(base, CONF_SUB), :] = (y * jax.nn.sigmoid(y)).astype(o_ref.dtype)
        return carry

    lax.fori_loop(0, tm // CONF_SUB, body, 0, unroll=8)


def _conformer(p, dw_w, dw_b, ln_w, ln_b, tm):
    L = p.shape[0]
    row = lambda i: (0, 0)
    return pl.pallas_call(
        _conformer_kernel,
        grid=(L // tm,),
        in_specs=[
            pl.BlockSpec((tm, CONV_CH), lambda i: (i, 0)),
            pl.BlockSpec((tm, CONV_CH), lambda i: (i, 1)),
            pl.BlockSpec((CONF_HALO, CONV_CH), row),
            pl.BlockSpec((1, CONV_CH), row),
            pl.BlockSpec((1, CONV_CH), row),
            pl.BlockSpec((1, CONV_CH), row),
        ],
        out_specs=pl.BlockSpec((tm, CONV_CH), lambda i: (i, 0)),
        out_shape=jax.ShapeDtypeStruct((L, CONV_CH), BF16),
        scratch_shapes=[pltpu.VMEM((tm + CONF_HALO, CONV_CH), F32),
                        pltpu.VMEM((SUBLANES, tm + CONF_HALO, CONV_CH), F32)],
        compiler_params=_cparams("arbitrary"),
        name="conformer_conv",
    )(p, p, dw_w, dw_b, ln_w, ln_b)


def _bmm(a, b):
    return lax.dot_general(a.astype(BF16), b.astype(BF16), (((2,), (1,)), ((0,), (0,))),
                           preferred_element_type=F32)


def _bmm_nt(a, b):
    return lax.dot_general(a.astype(BF16), b.astype(BF16), (((2,), (2,)), ((0,), (0,))),
                           preferred_element_type=F32)


def _unit_lower_inverse(low, row, col):
    n = low.shape[-1]
    eye = (row == col).astype(F32)

    def same_block(b):
        sh = b.bit_length() - 1
        return jnp.right_shift(row, sh) == jnp.right_shift(col, sh)

    base = SUBLANES
    a = jnp.where(same_block(base), low, 0.0)
    a2 = _bmm(a, a)
    a4 = _bmm(a2, a2)
    ia = eye - a
    p1 = ia + _bmm(ia, a2)
    t = p1 + _bmm(p1, a4)
    b = base
    while b < n:
        c = jnp.where(same_block(2 * b) & jnp.logical_not(same_block(b)), low, 0.0)
        t = t - _bmm(_bmm(t, c), t)
        b *= 2
    return t


def _gdn_kernel(q_ref, k_ref, v_ref, z_ref, ab_ref, cw_ref, alog_ref, dtb_ref, onw_ref,
                o_ref, ext_ref, s_ref, qkv_ref, gc_ref, gct_ref, beta_ref):
    tb = q_ref.shape[0]
    nc = tb // GDN_CHUNK
    c_ = GDN_CHUNK
    i = pl.program_id(0)

    @pl.when(i == 0)
    def _():
        ext_ref[0:SUBLANES, :] = jnp.zeros((SUBLANES, 3 * GDN_KEY), F32)
        s_ref[...] = jnp.zeros(s_ref.shape, F32)

    @pl.when(i > 0)
    def _():
        ext_ref[0:SUBLANES, :] = ext_ref[tb:tb + SUBLANES, :]

    ext_ref[SUBLANES:, 0:GDN_KEY] = q_ref[...].astype(F32)
    ext_ref[SUBLANES:, GDN_KEY:2 * GDN_KEY] = k_ref[...].astype(F32)
    ext_ref[SUBLANES:, 2 * GDN_KEY:] = v_ref[...].astype(F32)

    first = SUBLANES - (SHORT_CONV - 1)
    conv = ext_ref[pl.ds(first, tb), :] * cw_ref[0:1, :]
    for t in range(1, SHORT_CONV):
        conv = conv + ext_ref[pl.ds(first + t, tb), :] * cw_ref[t:t + 1, :]
    qkv = conv * jax.nn.sigmoid(conv)

    ab = ab_ref[...]
    g = -jnp.exp(alog_ref[...]) * jax.nn.softplus(ab + dtb_ref[...])
    beta = jax.nn.sigmoid(ab)

    rr = lax.broadcasted_iota(jnp.int32, (tb, tb), 0)
    cc = lax.broadcasted_iota(jnp.int32, (tb, tb), 1)
    csh = c_.bit_length() - 1
    tri = ((rr >= cc) & (jnp.right_shift(rr, csh) == jnp.right_shift(cc, csh))).astype(F32)
    gc = jnp.dot(tri, g, preferred_element_type=F32, precision=lax.Precision.HIGHEST)
    qkv_ref[...] = qkv
    gc_ref[...] = gc
    gct_ref[...] = gc.T
    beta_ref[...] = beta

    row = lax.broadcasted_iota(jnp.int32, (c_, c_), 0)
    col = lax.broadcasted_iota(jnp.int32, (c_, c_), 1)
    incl = row >= col
    strict = row > col

    pairs = [(c, h) for c in range(nc) for h in range(GDN_HEADS)]

    def stack(fn):
        return jnp.stack([fn(c * c_, h) for c, h in pairs])

    q = stack(lambda r0, h: qkv_ref[r0:r0 + c_, h * GDN_DK:(h + 1) * GDN_DK])
    k = stack(lambda r0, h: qkv_ref[r0:r0 + c_, GDN_KEY + h * GDN_DK:GDN_KEY + (h + 1) * GDN_DK])
    v = stack(lambda r0, h: qkv_ref[r0:r0 + c_, 2 * GDN_KEY + h * GDN_DK:2 * GDN_KEY + (h + 1) * GDN_DK])
    gcol = stack(lambda r0, h: gc_ref[r0:r0 + c_, h:h + 1])
    grow = stack(lambda r0, h: gct_ref[h:h + 1, r0:r0 + c_])
    bcol = stack(lambda r0, h: beta_ref[r0:r0 + c_, GDN_HEADS + h:GDN_HEADS + h + 1])
    glast = gcol[:, c_ - 1:c_, :]

    q = q * lax.rsqrt(jnp.sum(q * q, axis=-1, keepdims=True) + RMS_EPS) * (GDN_DK ** -0.5)
    k = k * lax.rsqrt(jnp.sum(k * k, axis=-1, keepdims=True) + RMS_EPS)
    decay = jnp.where(incl, jnp.exp(gcol - grow), 0.0)
    egc = jnp.exp(gcol)
    kb = k * bcol
    low = jnp.where(strict, _bmm_nt(kb, k) * decay, 0.0)
    tinv = _unit_lower_inverse(low, row, col)
    uw = _bmm(tinv, jnp.concatenate([v * bcol, kb * egc], axis=-1))
    qk = _bmm_nt(q, k) * decay
    ktt = jnp.swapaxes(k * jnp.exp(glast - gcol), 1, 2)
    lhs_state = jnp.concatenate([uw[:, :, GDN_DK:], q * egc], axis=1)
    lhs_vnew = jnp.concatenate([qk, ktt], axis=1)
    u = uw[:, :, :GDN_DK]
    sdecay = jnp.exp(glast)

    s = s_ref[...]
    for c in range(nc):
        sl = slice(c * GDN_HEADS, (c + 1) * GDN_HEADS)
        from_state = _bmm(lhs_state[sl], s)
        v_new = u[sl] - from_state[:, :c_]
        from_vnew = _bmm(lhs_vnew[sl], v_new)
        o = from_state[:, c_:] + from_vnew[:, :c_]
        s = s * sdecay[sl] + from_vnew[:, c_:]
        on = o * lax.rsqrt(jnp.mean(o * o, axis=-1, keepdims=True) + RMS_EPS) * onw_ref[...]
        r0 = c * c_
        for h in range(GDN_HEADS):
            zh = z_ref[r0:r0 + c_, h * GDN_DK:(h + 1) * GDN_DK].astype(F32)
            o_ref[r0:r0 + c_, h * GDN_DK:(h + 1) * GDN_DK] = (
                on[h] * (zh * jax.nn.sigmoid(zh))).astype(o_ref.dtype)
    s_ref[...] = s


def _gdn(p, ab, conv_w, alog_row, dtb_row, onorm_row, tb):
    L = p.shape[0]
    row = lambda i: (0, 0)
    return pl.pallas_call(
        _gdn_kernel,
        grid=(L // tb,),
        in_specs=[
            pl.BlockSpec((tb, GDN_KEY), lambda i: (i, 2)),
            pl.BlockSpec((tb, GDN_KEY), lambda i: (i, 3)),
            pl.BlockSpec((tb, GDN_KEY), lambda i: (i, 4)),
            pl.BlockSpec((tb, GDN_KEY), lambda i: (i, 5)),
            pl.BlockSpec((tb, LANES), lambda i: (i, 0)),
            pl.BlockSpec((SUBLANES, 3 * GDN_KEY), row),
            pl.BlockSpec((1, LANES), row),
            pl.BlockSpec((1, LANES), row),
            pl.BlockSpec((1, LANES), row),
        ],
        out_specs=pl.BlockSpec((tb, GDN_KEY), lambda i: (i, 0)),
        out_shape=jax.ShapeDtypeStruct((L, GDN_KEY), BF16),
        scratch_shapes=[pltpu.VMEM((tb + SUBLANES, 3 * GDN_KEY), F32),
                        pltpu.VMEM((GDN_HEADS, GDN_DK, GDN_DK), F32),
                        pltpu.VMEM((tb, 3 * GDN_KEY), F32),
                        pltpu.VMEM((tb, LANES), F32),
                        pltpu.VMEM((LANES, tb), F32),
                        pltpu.VMEM((tb, LANES), F32)],
        compiler_params=_cparams("arbitrary"),
        name="gated_deltanet",
    )(p, p, p, p, ab, conv_w, alog_row, dtb_row, onorm_row)


def _ffn_kernel(x_ref, ya_ref, yb_ref, wo_ref, nw_ref, wg_ref, wv_ref, cw_ref, wd_ref, fnw_ref,
                o_ref, h_ref, acc_ref, ubuf_a, ubuf_b, carry_ref, *, final_norm):
    tm = x_ref.shape[0]
    kh = ya_ref.shape[1]
    nv = tm // SUBLANES
    pitch = nv + SUBLANES
    halo = (FFN_CONV - 1) * SUBLANES
    i = pl.program_id(0)

    @pl.when(i == 0)
    def _():
        carry_ref[...] = jnp.zeros(carry_ref.shape, F32)

    x = (x_ref[...] + jnp.dot(ya_ref[...], wo_ref[0:kh, :], preferred_element_type=F32)
         + jnp.dot(yb_ref[...], wo_ref[kh:, :], preferred_element_type=F32))
    o_ref[...] = x
    ms = jnp.mean(x * x, axis=-1, keepdims=True)
    hn = x * lax.rsqrt(ms + RMS_EPS) * nw_ref[...]
    n_slab = acc_ref.shape[0]
    for s in range(SUBLANES):
        for l in range(n_slab):
            acc_ref[l, pitch * s:pitch * s + nv, :] = hn[nv * s:nv * (s + 1), l * LANES:(l + 1) * LANES]
    for v in range(0, nv, 2):
        for l in range(n_slab):
            pair = jnp.concatenate([acc_ref[l, pl.ds(v, SUBLANES, stride=pitch), :],
                                    acc_ref[l, pl.ds(v + 1, SUBLANES, stride=pitch), :]], axis=0)
            h_ref[SUBLANES * v:SUBLANES * (v + 2), l * LANES:(l + 1) * LANES] = pair.astype(BF16)
    acc_ref[:, 0:tm, :] = jnp.zeros((n_slab, tm, LANES), F32)
    sub = lax.broadcasted_iota(jnp.int32, (SUBLANES, FF_CHUNK), 0)

    def up(c, ubuf):
        h = h_ref[...]
        for gv, w_ref in enumerate((wg_ref, wv_ref)):
            u = jnp.dot(h, w_ref[c], preferred_element_type=F32)
            prev = carry_ref[c, gv]
            for g in range(FFN_CONV - 1):
                lo = tm - halo + SUBLANES * g
                wrapped = pltpu.roll(u[lo:lo + SUBLANES, :], 1, 0)
                top = prev[SUBLANES * g + SUBLANES - 1:SUBLANES * (g + 1), :]
                ubuf[gv, SUBLANES * g:SUBLANES * (g + 1), :] = jnp.where(sub == 0, top, wrapped)
            ubuf[gv, halo:, :] = u
            carry_ref[c, gv] = u[tm - halo:, :]

    def down(c, ubuf):
        cw = cw_ref[c]
        gate, val = [
            functools.reduce(jnp.add, [
                ubuf[gv, pl.ds(SUBLANES * t, tm), :] * cw[gv, t:t + 1, :]
                for t in range(FFN_CONV)])
            for gv in range(2)]
        act = (gate * jax.nn.sigmoid(gate) * val).astype(BF16)
        d = jnp.dot(act, wd_ref[c], preferred_element_type=F32)
        for l in range(n_slab):
            acc_ref[l, 0:tm, :] += d[:, l * LANES:(l + 1) * LANES]

    ubufs = (ubuf_a, ubuf_b)
    up(0, ubufs[0])
    for c in range(N_FF_CHUNKS):
        if c + 1 < N_FF_CHUNKS:
            up(c + 1, ubufs[(c + 1) % 2])
        down(c, ubufs[c % 2])
    runs = nv // SUBLANES
    for j in range(nv):
        first = SUBLANES * SUBLANES * (j % runs) + j // runs
        rows = slice(SUBLANES * j, SUBLANES * (j + 1))
        y = o_ref[rows, :] + jnp.concatenate(
            [acc_ref[l, pl.ds(first, SUBLANES, stride=SUBLANES), :] for l in range(n_slab)], axis=1)
        if final_norm:
            ms2 = jnp.mean(y * y, axis=-1, keepdims=True)
            y = y * lax.rsqrt(ms2 + RMS_EPS) * fnw_ref[...]
        o_ref[rows, :] = y


def _ffn(x, ya, ca, yb, cb, w_out, nw, wu, layer, cw, wd, fnw, tm, final_norm):
    L, D = x.shape
    kh = w_out.shape[0] // 2
    halo = (FFN_CONV - 1) * SUBLANES
    assert tm % (SUBLANES * SUBLANES) == 0
    const2 = lambda i: (0, 0)
    resident = dict(pipeline_mode=pl.Buffered(1))
    return pl.pallas_call(
        functools.partial(_ffn_kernel, final_norm=final_norm),
        grid=(L // tm,),
        in_specs=[
            pl.BlockSpec((tm, D), lambda i: (i, 0)),
            pl.BlockSpec((tm, kh), lambda i: (i, ca)),
            pl.BlockSpec((tm, kh), lambda i: (i, cb)),
            pl.BlockSpec((2 * kh, D), const2, **resident),
            pl.BlockSpec((1, D), const2),
            pl.BlockSpec((None, N_FF_CHUNKS, D, FF_CHUNK), lambda i: (layer, 0, 0, 0), **resident),
            pl.BlockSpec((None, N_FF_CHUNKS, D, FF_CHUNK), lambda i: (layer, 1, 0, 0), **resident),
            pl.BlockSpec((N_FF_CHUNKS, 2, SUBLANES, FF_CHUNK), lambda i: (0, 0, 0, 0)),
            pl.BlockSpec((N_FF_CHUNKS, FF_CHUNK, D), lambda i: (0, 0, 0), **resident),
            pl.BlockSpec((1, D), const2),
        ],
        out_specs=pl.BlockSpec((tm, D), lambda i: (i, 0)),
        out_shape=jax.ShapeDtypeStruct((L, D), F32),
        scratch_shapes=[pltpu.VMEM((tm, D), BF16),
                        pltpu.VMEM((D // LANES, tm + SUBLANES * SUBLANES, LANES), F32),
                        pltpu.VMEM((2, tm + halo, FF_CHUNK), F32),
                        pltpu.VMEM((2, tm + halo, FF_CHUNK), F32),
                        pltpu.VMEM((N_FF_CHUNKS, 2, halo, FF_CHUNK), F32)],
        compiler_params=_cparams("arbitrary"),
        name="conv_ffn",
    )(x, ya, yb, w_out, nw, wu, wu, cw, wd, fnw)


def _chunk_weights_kernel(w_ref, o_ref):
    for g in range(o_ref.shape[1]):
        o_ref[0, g] = w_ref[0, :, g * FF_CHUNK:(g + 1) * FF_CHUNK].astype(o_ref.dtype)


def _chunk_up_weights(w_up, per_step=2):
    nl, d, n = w_up.shape
    nchunk = n // FF_CHUNK
    return pl.pallas_call(
        _chunk_weights_kernel,
        grid=(nl, nchunk // per_step),
        in_specs=[pl.BlockSpec((1, d, per_step * FF_CHUNK), lambda l, c: (l, 0, c))],
        out_specs=pl.BlockSpec((1, per_step, d, FF_CHUNK), lambda l, c: (l, c, 0, 0)),
        out_shape=jax.ShapeDtypeStruct((nl, nchunk, d, FF_CHUNK), BF16),
        compiler_params=_cparams("arbitrary", "arbitrary"),
        name="chunk_up_weights",
    )(w_up)


def _softplus2(z):
    return jnp.where(z > 30.0, z, jnp.log2(1.0 + jnp.exp2(jnp.minimum(z, 30.0))))


def _attn_kernel(q_ref, k_ref, v_ref, u_ref, o_ref):
    tq = q_ref.shape[0]
    tk = u_ref.shape[0]
    qi = pl.program_id(1)
    n_diag = tq // tk
    n_pair = q_ref.shape[1] // LANES
    lane = lax.broadcasted_iota(jnp.int32, (1, LANES), 1)
    umat = u_ref[...]
    rows = 2 * tq
    rowq = jnp.bitwise_and(lax.broadcasted_iota(jnp.int32, (rows, tk), 0), tq - 1)
    colk = lax.broadcasted_iota(jnp.int32, (rows, tk), 1)

    qs = []
    for p in range(n_pair):
        q = q_ref[:, p * LANES:(p + 1) * LANES]
        zero = jnp.zeros_like(q)
        qs.append(jnp.concatenate([jnp.where(lane < SB_HD, q, zero),
                                   jnp.where(lane >= SB_HD, q, zero)], axis=0))

    def tile(kb, runs, accs, diag_offset, bias=None):
        start = pl.multiple_of(kb * tk, tk)
        zs, sps = [], []
        for p in range(n_pair):
            k = k_ref[pl.ds(start, tk), p * LANES:(p + 1) * LANES]
            z = lax.dot_general(qs[p], k, (((1,), (1,)), ((), ())), preferred_element_type=F32)
            if bias is not None:
                z = z + bias
            sp = _softplus2(z)
            if diag_offset is not None:
                valid = (colk + diag_offset) < rowq
                sp = jnp.where(valid, sp, 0.0)
            zs.append(z)
            sps.append(sp.astype(BF16))
        cum_all = jnp.dot(jnp.concatenate(sps, axis=0), umat, preferred_element_type=F32)
        new_runs, new_accs = [], []
        for p in range(n_pair):
            v = v_ref[pl.ds(start, tk), p * LANES:(p + 1) * LANES]
            cum = cum_all[p * rows:(p + 1) * rows]
            att = jnp.exp2(zs[p] - cum - runs[p])
            if diag_offset is not None:
                att = jnp.where(valid, att, 0.0)
            new_accs.append(accs[p] + jnp.dot(att.astype(BF16), v, preferred_element_type=F32))
            new_runs.append(runs[p] + cum[:, 0:1])
        return tuple(new_runs), tuple(new_accs)

    assert tq & (tq - 1) == 0
    runs = tuple(jnp.zeros((rows, 1), F32) for _ in range(n_pair))
    accs = tuple(jnp.zeros((rows, LANES), F32) for _ in range(n_pair))
    for d in range(n_diag):
        dd = n_diag - 1 - d
        runs, accs = tile(qi * n_diag + dd, runs, accs, dd * tk)
    has_left = qi * n_diag >= 1
    runs, accs = tile(jnp.maximum(qi * n_diag - 1, 0), runs, accs, None,
                      bias=jnp.where(has_left, 0.0, ATT_OFF).astype(F32))
    kb0 = qi * n_diag - 2

    def cond(state):
        kb, runs, _ = state
        least = functools.reduce(jnp.minimum, runs)
        return (kb >= 0) & (jnp.min(least) < ATT_STOP)

    def body(state):
        kb, runs, accs = state
        runs, accs = tile(kb, runs, accs, None)
        return kb - 1, runs, accs

    _, _, accs = lax.while_loop(cond, body, (kb0, runs, accs))
    for p in range(n_pair):
        o_ref[:, p * LANES:(p + 1) * LANES] = jnp.where(
            lane < SB_HD, accs[p][:tq], accs[p][tq:]).astype(o_ref.dtype)


def _attention(qkv, umat, tq, pairs_per_step):
    L = qkv.shape[0]
    n_pairs = SB_HEADS * SB_HD // LANES
    n_steps = n_pairs // pairs_per_step
    width = pairs_per_step * LANES
    tk = umat.shape[0]
    return pl.pallas_call(
        _attn_kernel,
        grid=(n_steps, L // tq),
        in_specs=[
            pl.BlockSpec((tq, width), lambda hp, i: (i, hp)),
            pl.BlockSpec((L, width), lambda hp, i: (0, n_steps + hp), pipeline_mode=pl.Buffered(1)),
            pl.BlockSpec((L, width), lambda hp, i: (0, 2 * n_steps + hp), pipeline_mode=pl.Buffered(1)),
            pl.BlockSpec((tk, tk), lambda hp, i: (0, 0)),
        ],
        out_specs=pl.BlockSpec((tq, width), lambda hp, i: (i, hp)),
        out_shape=jax.ShapeDtypeStruct((L, SB_HEADS * SB_HD), BF16),
        compiler_params=_cparams("arbitrary", "arbitrary"),
        name="stickbreak_attention",
    )(qkv, qkv, qkv, umat)


def _row(v, width=None):
    v = v.astype(F32).reshape(1, -1)
    if width is not None and v.shape[1] < width:
        v = jnp.pad(v, ((0, 0), (0, width - v.shape[1])))
    return v


def _pad_rows(w, rows):
    return jnp.pad(w.astype(F32), ((0, rows - w.shape[0]), (0, 0)))


def _ffn_small_weights(conv_w, w_down):
    d = w_down.shape[1]
    cw = _pad_rows(conv_w, SUBLANES).reshape(SUBLANES, 2, N_FF_CHUNKS, FF_CHUNK).transpose(2, 1, 0, 3)
    wd = w_down.astype(BF16).reshape(N_FF_CHUNKS, FF_CHUNK, d)
    return cw, wd


def kernel(x, mix0_norm, w_in0, a_dw_w, a_dw_b, a_ln_w, a_ln_b, gdn_conv_w, gdn_a_log,
           gdn_dt_bias, gdn_onorm_w, w_out0, mix1_norm, w_qkv1, w_out1,
           ffn_norm, w_up, ffn_conv_w, w_down, final_norm):
    B, L, D = x.shape
    assert B == 1 and D == D_MODEL
    xs = x.reshape(L, D)

    w_in = w_in0[0].astype(BF16)
    w_main = w_in[:, :IN0_MAIN]
    w_ab = jnp.pad(w_in[:, IN0_MAIN:], ((0, 0), (0, LANES - (w_in.shape[1] - IN0_MAIN))))
    p, ab = _rms_matmul(xs, _row(mix0_norm[0]), w_main, jnp.ones((1, IN0_MAIN), F32), BF16,
                        tm=PROJ_TM, tn=768, w_side=w_ab)
    wu = _chunk_up_weights(w_up)
    y_a = _conformer(p, _pad_rows(a_dw_w[0], CONF_HALO), _row(a_dw_b[0]), _row(a_ln_w[0]),
                     _row(a_ln_b[0]), tm=512)
    y_b = _gdn(p, ab, _pad_rows(gdn_conv_w[0], SUBLANES), _row(gdn_a_log[0], LANES),
               _row(gdn_dt_bias[0], LANES), _row(gdn_onorm_w[0]), tb=4 * GDN_CHUNK)
    cw, wd = _ffn_small_weights(ffn_conv_w[0], w_down[0])
    xs = _ffn(xs, y_a, 0, y_b, 0, w_out0[0].astype(BF16), _row(ffn_norm[0]), wu, 0, cw, wd,
              _row(final_norm), tm=FFN_TM, final_norm=False)

    hd = SB_HEADS * SB_HD
    qscale = jnp.concatenate([jnp.full((1, hd), SB_HD ** -0.5 * LOG2E, F32),
                              jnp.ones((1, 2 * hd), F32)], axis=1)
    qkv = _rms_matmul(xs, _row(mix1_norm[0]), w_qkv1[0].astype(BF16), qscale, BF16, tm=PROJ_TM, tn=768)
    ki = lax.broadcasted_iota(jnp.int32, (ATT_TK, ATT_TK), 0)
    si = lax.broadcasted_iota(jnp.int32, (ATT_TK, ATT_TK), 1)
    umat = (ki >= si).astype(BF16)
    o = _attention(qkv, umat, tq=ATT_TQ, pairs_per_step=ATT_PAIRS)
    cw, wd = _ffn_small_weights(ffn_conv_w[1], w_down[1])
    xs = _ffn(xs, o, 0, o, 1, w_out1[0].astype(BF16), _row(ffn_norm[1]), wu, 1, cw, wd,
              _row(final_norm), tm=FFN_TM, final_norm=True)
    return xs.reshape(B, L, D)
```

```python
import functools
import math

import jax
import jax.numpy as jnp
from jax import lax
from jax.experimental import pallas as pl
from jax.experimental.pallas import tpu as pltpu

F32 = jnp.float32
BF16 = jnp.bfloat16

D_MODEL = 1024
RMS_EPS = 1e-6
LN_EPS = 1e-5
CONV_CH = 512
CONV_WIDTH = 31
GDN_HEADS = 4
GDN_DK = 128
GDN_KEY = GDN_HEADS * GDN_DK
SHORT_CONV = 4
SB_HEADS = 16
SB_HD = 64
D_FF = 2816
FFN_CONV = 3

LANES = 128
SUBLANES = 8
VMEM_LIMIT = 60 * 1024 * 1024

IN0_MAIN = 3072
FF_CHUNK = 256
N_FF_CHUNKS = D_FF // FF_CHUNK
FFN_TM = 1024
PROJ_TM = 1024
GDN_CHUNK = 128
CONF_HALO = 32
CONF_SUB = 32
ATT_TQ = 256
ATT_TK = 256
ATT_PAIRS = 4
LOG2E = 1.4426950408889634
ATT_STOP = 160.0
ATT_OFF = -1e30


def _cparams(*sem):
    return pltpu.CompilerParams(dimension_semantics=sem, vmem_limit_bytes=VMEM_LIMIT)


def _bdot(a, b):
    return jnp.dot(a.astype(BF16), b.astype(BF16), preferred_element_type=F32)


def _bdot_nt(a, b):
    return lax.dot_general(a.astype(BF16), b.astype(BF16), (((1,), (1,)), ((), ())),
                           preferred_element_type=F32)


def _rms_matmul_kernel(*refs, has_side, tn):
    if has_side:
        x_ref, nw_ref, w_ref, cs_ref, ws_ref, o_ref, os_ref, h_ref = refs
    else:
        x_ref, nw_ref, w_ref, cs_ref, o_ref, h_ref = refs

    x = x_ref[...]
    ms = jnp.mean(x * x, axis=-1, keepdims=True)
    h_ref[...] = (x * lax.rsqrt(ms + RMS_EPS) * nw_ref[...]).astype(BF16)
    if has_side:
        os_ref[...] = jnp.dot(h_ref[...], ws_ref[...], preferred_element_type=F32)
    for n in range(o_ref.shape[1] // tn):
        cols = slice(n * tn, (n + 1) * tn)
        acc = jnp.dot(h_ref[...], w_ref[:, cols], preferred_element_type=F32)
        o_ref[:, cols] = (acc * cs_ref[:, cols]).astype(o_ref.dtype)


def _rms_matmul(x, nw, w, colscale, out_dtype, tm, tn, w_side=None):
    L, D = x.shape
    N = colscale.shape[1]
    has_side = w_side is not None
    const = lambda i: (0, 0)
    in_specs = [
        pl.BlockSpec((tm, D), lambda i: (i, 0)),
        pl.BlockSpec((1, D), const),
        pl.BlockSpec((D, N), const, pipeline_mode=pl.Buffered(1)),
        pl.BlockSpec((1, N), const),
    ]
    out_specs = [pl.BlockSpec((tm, N), lambda i: (i, 0))]
    out_shape = [jax.ShapeDtypeStruct((L, N), out_dtype)]
    args = [x, nw, w, colscale]
    if has_side:
        ns = w_side.shape[1]
        in_specs.append(pl.BlockSpec((D, ns), const))
        out_specs.append(pl.BlockSpec((tm, ns), lambda i: (i, 0)))
        out_shape.append(jax.ShapeDtypeStruct((L, ns), F32))
        args.append(w_side)
    outs = pl.pallas_call(
        functools.partial(_rms_matmul_kernel, has_side=has_side, tn=tn),
        grid=(L // tm,),
        in_specs=in_specs,
        out_specs=out_specs,
        out_shape=out_shape,
        scratch_shapes=[pltpu.VMEM((tm, D), BF16)],
        compiler_params=_cparams("arbitrary"),
        name="rms_matmul",
    )(*args)
    return outs if has_side else outs[0]


def _conformer_kernel(val_ref, gate_ref, w_ref, b_ref, lnw_ref, lnb_ref, o_ref, ext_ref, sh_ref):
    tm = val_ref.shape[0]
    i = pl.program_id(0)

    @pl.when(i == 0)
    def _():
        ext_ref[0:CONF_HALO, :] = jnp.zeros((CONF_HALO, CONV_CH), F32)

    @pl.when(i > 0)
    def _():
        ext_ref[0:CONF_HALO, :] = ext_ref[tm:tm + CONF_HALO, :]

    ext_ref[CONF_HALO:, :] = val_ref[...].astype(F32) * jax.nn.sigmoid(gate_ref[...].astype(F32))

    nrows = tm + CONF_HALO - SUBLANES
    for b in range(1, SUBLANES):
        sh_ref[b, 0:nrows, :] = ext_ref[pl.ds(b, nrows), :]

    first = CONF_HALO - (CONV_WIDTH - 1)

    def body(r, carry):
        base = pl.multiple_of(r * CONF_SUB, CONF_SUB)
        acc = jnp.zeros((CONF_SUB, CONV_CH), F32)
        for k in range(CONV_WIDTH):
            a, b = divmod(first + k, SUBLANES)
            src = ext_ref if b == 0 else sh_ref.at[b]
            acc = acc + src[pl.ds(base + SUBLANES * a, CONF_SUB), :] * w_ref[k:k + 1, :]
        acc = acc + b_ref[...]
        mu = jnp.mean(acc, axis=-1, keepdims=True)
        xc = acc - mu
        var = jnp.mean(xc * xc, axis=-1, keepdims=True)
        y = xc * lax.rsqrt(var + LN_EPS) * lnw_ref[...] + lnb_ref[...]
        o_ref[pl.ds(base, CONF_SUB), :] = (y * jax.nn.sigmoid(y)).astype(o_ref.dtype)
        return carry

    lax.fori_loop(0, tm // CONF_SUB, body, 0, unroll=8)


def _conformer(p, dw_w, dw_b, ln_w, ln_b, tm):
    L = p.shape[0]
    row = lambda i: (0, 0)
    return pl.pallas_call(
        _conformer_kernel,
        grid=(L // tm,),
        in_specs=[
            pl.BlockSpec((tm, CONV_CH), lambda i: (i, 0)),
            pl.BlockSpec((tm, CONV_CH), lambda i: (i, 1)),
            pl.BlockSpec((CONF_HALO, CONV_CH), row),
            pl.BlockSpec((1, CONV_CH), row),
            pl.BlockSpec((1, CONV_CH), row),
            pl.BlockSpec((1, CONV_CH), row),
        ],
        out_specs=pl.BlockSpec((tm, CONV_CH), lambda i: (i, 0)),
        out_shape=jax.ShapeDtypeStruct((L, CONV_CH), BF16),
        scratch_shapes=[pltpu.VMEM((tm + CONF_HALO, CONV_CH), F32),
                        pltpu.VMEM((SUBLANES, tm + CONF_HALO, CONV_CH), F32)],
        compiler_params=_cparams("arbitrary"),
        name="conformer_conv",
    )(p, p, dw_w, dw_b, ln_w, ln_b)


def _bmm(a, b):
    return lax.dot_general(a.astype(BF16), b.astype(BF16), (((2,), (1,)), ((0,), (0,))),
                           preferred_element_type=F32)


def _bmm_nt(a, b):
    return lax.dot_general(a.astype(BF16), b.astype(BF16), (((2,), (2,)), ((0,), (0,))),
                           preferred_element_type=F32)


def _unit_lower_inverse(low, row, col):
    n = low.shape[-1]
    eye = (row == col).astype(F32)

    def same_block(b):
        sh = b.bit_length() - 1
        return jnp.right_shift(row, sh) == jnp.right_shift(col, sh)

    base = SUBLANES
    a = jnp.where(same_block(base), low, 0.0)
    a2 = _bmm(a, a)
    a4 = _bmm(a2, a2)
    ia = eye - a
    p1 = ia + _bmm(ia, a2)
    t = p1 + _bmm(p1, a4)
    b = base
    while b < n:
        c = jnp.where(same_block(2 * b) & jnp.logical_not(same_block(b)), low, 0.0)
        t = t - _bmm(_bmm(t, c), t)
        b *= 2
    return t


def _gdn_kernel(q_ref, k_ref, v_ref, z_ref, ab_ref, cw_ref, alog_ref, dtb_ref, onw_ref,
                o_ref, ext_ref, s_ref, qkv_ref, gc_ref, gct_ref, beta_ref):
    tb = q_ref.shape[0]
    nc = tb // GDN_CHUNK
    c_ = GDN_CHUNK
    i = pl.program_id(0)

    @pl.when(i == 0)
    def _():
        ext_ref[0:SUBLANES, :] = jnp.zeros((SUBLANES, 3 * GDN_KEY), F32)
        s_ref[...] = jnp.zeros(s_ref.shape, F32)

    @pl.when(i > 0)
    def _():
        ext_ref[0:SUBLANES, :] = ext_ref[tb:tb + SUBLANES, :]

    ext_ref[SUBLANES:, 0:GDN_KEY] = q_ref[...].astype(F32)
    ext_ref[SUBLANES:, GDN_KEY:2 * GDN_KEY] = k_ref[...].astype(F32)
    ext_ref[SUBLANES:, 2 * GDN_KEY:] = v_ref[...].astype(F32)

    first = SUBLANES - (SHORT_CONV - 1)
    conv = ext_ref[pl.ds(first, tb), :] * cw_ref[0:1, :]
    for t in range(1, SHORT_CONV):
        conv = conv + ext_ref[pl.ds(first + t, tb), :] * cw_ref[t:t + 1, :]
    qkv = conv * jax.nn.sigmoid(conv)

    ab = ab_ref[...]
    g = -jnp.exp(alog_ref[...]) * jax.nn.softplus(ab + dtb_ref[...])
    beta = jax.nn.sigmoid(ab)

    rr = lax.broadcasted_iota(jnp.int32, (tb, tb), 0)
    cc = lax.broadcasted_iota(jnp.int32, (tb, tb), 1)
    csh = c_.bit_length() - 1
    tri = ((rr >= cc) & (jnp.right_shift(rr, csh) == jnp.right_shift(cc, csh))).astype(F32)
    gc = jnp.dot(tri, g, preferred_element_type=F32, precision=lax.Precision.HIGHEST)
    qkv_ref[...] = qkv
    gc_ref[...] = gc
    gct_ref[...] = gc.T
    beta_ref[...] = beta

    row = lax.broadcasted_iota(jnp.int32, (c_, c_), 0)
    col = lax.broadcasted_iota(jnp.int32, (c_, c_), 1)
    incl = row >= col
    strict = row > col

    pairs = [(c, h) for c in range(nc) for h in range(GDN_HEADS)]

    def stack(fn):
        return jnp.stack([fn(c * c_, h) for c, h in pairs])

    q = stack(lambda r0, h: qkv_ref[r0:r0 + c_, h * GDN_DK:(h + 1) * GDN_DK])
    k = stack(lambda r0, h: qkv_ref[r0:r0 + c_, GDN_KEY + h * GDN_DK:GDN_KEY + (h + 1) * GDN_DK])
    v = stack(lambda r0, h: qkv_ref[r0:r0 + c_, 2 * GDN_KEY + h * GDN_DK:2 * GDN_KEY + (h + 1) * GDN_DK])
    gcol = stack(lambda r0, h: gc_ref[r0:r0 + c_, h:h + 1])
    grow = stack(lambda r0, h: gct_ref[h:h + 1, r0:r0 + c_])
    bcol = stack(lambda r0, h: beta_ref[r0:r0 + c_, GDN_HEADS + h:GDN_HEADS + h + 1])
    glast = gcol[:, c_ - 1:c_, :]

    q = q * lax.rsqrt(jnp.sum(q * q, axis=-1, keepdims=True) + RMS_EPS) * (GDN_DK ** -0.5)
    k = k * lax.rsqrt(jnp.sum(k * k, axis=-1, keepdims=True) + RMS_EPS)
    decay = jnp.where(incl, jnp.exp(gcol - grow), 0.0)
    egc = jnp.exp(gcol)
    kb = k * bcol
    low = jnp.where(strict, _bmm_nt(kb, k) * decay, 0.0)
    tinv = _unit_lower_inverse(low, row, col)
    uw = _bmm(tinv, jnp.concatenate([v * bcol, kb * egc], axis=-1))
    qk = _bmm_nt(q, k) * decay
    ktt = jnp.swapaxes(k * jnp.exp(glast - gcol), 1, 2)
    lhs_state = jnp.concatenate([uw[:, :, GDN_DK:], q * egc], axis=1)
    lhs_vnew = jnp.concatenate([qk, ktt], axis=1)
    u = uw[:, :, :GDN_DK]
    sdecay = jnp.exp(glast)

    s = s_ref[...]
    for c in range(nc):
        sl = slice(c * GDN_HEADS, (c + 1) * GDN_HEADS)
        from_state = _bmm(lhs_state[sl], s)
        v_new = u[sl] - from_state[:, :c_]
        from_vnew = _bmm(lhs_vnew[sl], v_new)
        o = from_state[:, c_:] + from_vnew[:, :c_]
        s = s * sdecay[sl] + from_vnew[:, c_:]
        on = o * lax.rsqrt(jnp.mean(o * o, axis=-1, keepdims=True) + RMS_EPS) * onw_ref[...]
        r0 = c * c_
        for h in range(GDN_HEADS):
            zh = z_ref[r0:r0 + c_, h * GDN_DK:(h + 1) * GDN_DK].astype(F32)
            o_ref[r0:r0 + c_, h * GDN_DK:(h + 1) * GDN_DK] = (
                on[h] * (zh * jax.nn.sigmoid(zh))).astype(o_ref.dtype)
    s_ref[...] = s


def _gdn(p, ab, conv_w, alog_row, dtb_row, onorm_row, tb):
    L = p.shape[0]
    row = lambda i: (0, 0)
    return pl.pallas_call(
        _gdn_kernel,
        grid=(L // tb,),
        in_specs=[
            pl.BlockSpec((tb, GDN_KEY), lambda i: (i, 2)),
            pl.BlockSpec((tb, GDN_KEY), lambda i: (i, 3)),
            pl.BlockSpec((tb, GDN_KEY), lambda i: (i, 4)),
            pl.BlockSpec((tb, GDN_KEY), lambda i: (i, 5)),
            pl.BlockSpec((tb, LANES), lambda i: (i, 0)),
            pl.BlockSpec((SUBLANES, 3 * GDN_KEY), row),
            pl.BlockSpec((1, LANES), row),
            pl.BlockSpec((1, LANES), row),
            pl.BlockSpec((1, LANES), row),
        ],
        out_specs=pl.BlockSpec((tb, GDN_KEY), lambda i: (i, 0)),
        out_shape=jax.ShapeDtypeStruct((L, GDN_KEY), BF16),
        scratch_shapes=[pltpu.VMEM((tb + SUBLANES, 3 * GDN_KEY), F32),
                        pltpu.VMEM((GDN_HEADS, GDN_DK, GDN_DK), F32),
                        pltpu.VMEM((tb, 3 * GDN_KEY), F32),
                        pltpu.VMEM((tb, LANES), F32),
                        pltpu.VMEM((LANES, tb), F32),
                        pltpu.VMEM((tb, LANES), F32)],
        compiler_params=_cparams("arbitrary"),
        name="gated_deltanet",
    )(p, p, p, p, ab, conv_w, alog_row, dtb_row, onorm_row)


def _ffn_kernel(x_ref, ya_ref, yb_ref, wo_ref, nw_ref, wg_ref, wv_ref, cw_ref, wd_ref, fnw_ref,
                o_ref, h_ref, acc_ref, ubuf_a, ubuf_b, carry_ref, *, final_norm):
    tm = x_ref.shape[0]
    kh = ya_ref.shape[1]
    nv = tm // SUBLANES
    pitch = nv + SUBLANES
    halo = (FFN_CONV - 1) * SUBLANES
    i = pl.program_id(0)

    @pl.when(i == 0)
    def _():
        carry_ref[...] = jnp.zeros(carry_ref.shape, F32)

    x = (x_ref[...] + jnp.dot(ya_ref[...], wo_ref[0:kh, :], preferred_element_type=F32)
         + jnp.dot(yb_ref[...], wo_ref[kh:, :], preferred_element_type=F32))
    o_ref[...] = x
    ms = jnp.mean(x * x, axis=-1, keepdims=True)
    hn = x * lax.rsqrt(ms + RMS_EPS) * nw_ref[...]
    n_slab = acc_ref.shape[0]
    for s in range(SUBLANES):
        for l in range(n_slab):
            acc_ref[l, pitch * s:pitch * s + nv, :] = hn[nv * s:nv * (s + 1), l * LANES:(l + 1) * LANES]
    for v in range(0, nv, 2):
        for l in range(n_slab):
            pair = jnp.concatenate([acc_ref[l, pl.ds(v, SUBLANES, stride=pitch), :],
                                    acc_ref[l, pl.ds(v + 1, SUBLANES, stride=pitch), :]], axis=0)
            h_ref[SUBLANES * v:SUBLANES * (v + 2), l * LANES:(l + 1) * LANES] = pair.astype(BF16)
    acc_ref[:, 0:tm, :] = jnp.zeros((n_slab, tm, LANES), F32)
    sub = lax.broadcasted_iota(jnp.int32, (SUBLANES, FF_CHUNK), 0)

    def up(c, ubuf):
        h = h_ref[...]
        for gv, w_ref in enumerate((wg_ref, wv_ref)):
            u = jnp.dot(h, w_ref[c], preferred_element_type=F32)
            prev = carry_ref[c, gv]
            for g in range(FFN_CONV - 1):
                lo = tm - halo + SUBLANES * g
                wrapped = pltpu.roll(u[lo:lo + SUBLANES, :], 1, 0)
                top = prev[SUBLANES * g + SUBLANES - 1:SUBLANES * (g + 1), :]
                ubuf[gv, SUBLANES * g:SUBLANES * (g + 1), :] = jnp.where(sub == 0, top, wrapped)
            ubuf[gv, halo:, :] = u
            carry_ref[c, gv] = u[tm - halo:, :]

    def down(c, ubuf):
        cw = cw_ref[c]
        gate, val = [
            functools.reduce(jnp.add, [
                ubuf[gv, pl.ds(SUBLANES * t, tm), :] * cw[gv, t:t + 1, :]
                for t in range(FFN_CONV)])
            for gv in range(2)]
        act = (gate * jax.nn.sigmoid(gate) * val).astype(BF16)
        d = jnp.dot(act, wd_ref[c], preferred_element_type=F32)
        for l in range(n_slab):
            acc_ref[l, 0:tm, :] += d[:, l * LANES:(l + 1) * LANES]

    ubufs = (ubuf_a, ubuf_b)
    up(0, ubufs[0])
    for c in range(N_FF_CHUNKS):
        if c + 1 < N_FF_CHUNKS:
            up(c + 1, ubufs[(c + 1) % 2])
        down(c, ubufs[c % 2])
    runs = nv // SUBLANES
    for j in range(nv):
        first = SUBLANES * SUBLANES * (j % runs) + j // runs
        rows = slice(SUBLANES * j, SUBLANES * (j + 1))
        y = o_ref[rows, :] + jnp.concatenate(
            [acc_ref[l, pl.ds(first, SUBLANES, stride=SUBLANES), :] for l in range(n_slab)], axis=1)
        if final_norm:
            ms2 = jnp.mean(y * y, axis=-1, keepdims=True)
            y = y * lax.rsqrt(ms2 + RMS_EPS) * fnw_ref[...]
        o_ref[rows, :] = y


def _ffn(x, ya, ca, yb, cb, w_out, nw, wu, layer, cw, wd, fnw, tm, final_norm):
    L, D = x.shape
    kh = w_out.shape[0] // 2
    halo = (FFN_CONV - 1) * SUBLANES
    assert tm % (SUBLANES * SUBLANES) == 0
    const2 = lambda i: (0, 0)
    resident = dict(pipeline_mode=pl.Buffered(1))
    return pl.pallas_call(
        functools.partial(_ffn_kernel, final_norm=final_norm),
        grid=(L // tm,),
        in_specs=[
            pl.BlockSpec((tm, D), lambda i: (i, 0)),
            pl.BlockSpec((tm, kh), lambda i: (i, ca)),
            pl.BlockSpec((tm, kh), lambda i: (i, cb)),
            pl.BlockSpec((2 * kh, D), const2, **resident),
            pl.BlockSpec((1, D), const2),
            pl.BlockSpec((None, N_FF_CHUNKS, D, FF_CHUNK), lambda i: (layer, 0, 0, 0), **resident),
            pl.BlockSpec((None, N_FF_CHUNKS, D, FF_CHUNK), lambda i: (layer, 1, 0, 0), **resident),
            pl.BlockSpec((N_FF_CHUNKS, 2, SUBLANES, FF_CHUNK), lambda i: (0, 0, 0, 0)),
            pl.BlockSpec((None, N_FF_CHUNKS, FF_CHUNK, D), lambda i: (layer, 0, 0, 0), **resident),
            pl.BlockSpec((1, D), const2),
        ],
        out_specs=pl.BlockSpec((tm, D), lambda i: (i, 0)),
        out_shape=jax.ShapeDtypeStruct((L, D), F32),
        scratch_shapes=[pltpu.VMEM((tm, D), BF16),
                        pltpu.VMEM((D // LANES, tm + SUBLANES * SUBLANES, LANES), F32),
                        pltpu.VMEM((2, tm + halo, FF_CHUNK), F32),
                        pltpu.VMEM((2, tm + halo, FF_CHUNK), F32),
                        pltpu.VMEM((N_FF_CHUNKS, 2, halo, FF_CHUNK), F32)],
        compiler_params=_cparams("arbitrary"),
        name="conv_ffn",
    )(x, ya, yb, w_out, nw, wu, wu, cw, wd, fnw)


def _chunk_weights_kernel(w_ref, o_ref):
    for g in range(o_ref.shape[1]):
        o_ref[0, g] = w_ref[0, :, g * FF_CHUNK:(g + 1) * FF_CHUNK].astype(o_ref.dtype)


def _chunk_up_weights(w_up, per_step=2):
    nl, d, n = w_up.shape
    nchunk = n // FF_CHUNK
    return pl.pallas_call(
        _chunk_weights_kernel,
        grid=(nl, nchunk // per_step),
        in_specs=[pl.BlockSpec((1, d, per_step * FF_CHUNK), lambda l, c: (l, 0, c))],
        out_specs=pl.BlockSpec((1, per_step, d, FF_CHUNK), lambda l, c: (l, c, 0, 0)),
        out_shape=jax.ShapeDtypeStruct((nl, nchunk, d, FF_CHUNK), BF16),
        compiler_params=_cparams("arbitrary", "arbitrary"),
        name="chunk_up_weights",
    )(w_up)


def _softplus2(z):
    return jnp.maximum(jnp.log2(1.0 + jnp.exp2(jnp.minimum(z, 126.0))), z)


def _attn_kernel(q_ref, k_ref, v_ref, u_ref, o_ref):
    tq = q_ref.shape[0]
    tk = u_ref.shape[0]
    qi = pl.program_id(1)
    n_diag = tq // tk
    n_pair = q_ref.shape[1] // LANES
    lane = lax.broadcasted_iota(jnp.int32, (1, LANES), 1)
    umat = u_ref[...]
    rows = 2 * tq
    rowq = jnp.bitwise_and(lax.broadcasted_iota(jnp.int32, (rows, tk), 0), tq - 1)
    colk = lax.broadcasted_iota(jnp.int32, (rows, tk), 1)

    qs = []
    for p in range(n_pair):
        q = q_ref[:, p * LANES:(p + 1) * LANES]
        zero = jnp.zeros_like(q)
        qs.append(jnp.concatenate([jnp.where(lane < SB_HD, q, zero),
                                   jnp.where(lane >= SB_HD, q, zero)], axis=0))

    def tile(kb, runs, accs, diag_offset, bias=None):
        start = pl.multiple_of(kb * tk, tk)
        zs, sps = [], []
        for p in range(n_pair):
            k = k_ref[pl.ds(start, tk), p * LANES:(p + 1) * LANES]
            z = lax.dot_general(qs[p], k, (((1,), (1,)), ((), ())), preferred_element_type=F32)
            if bias is not None:
                z = z + bias
            sp = _softplus2(z)
            if diag_offset is not None:
                valid = (colk + diag_offset) < rowq
                sp = jnp.where(valid, sp, 0.0)
            zs.append(z)
            sps.append(sp.astype(BF16))
        cum_all = jnp.dot(jnp.concatenate(sps, axis=0), umat, preferred_element_type=F32)
        new_runs, new_accs = [], []
        for p in range(n_pair):
            v = v_ref[pl.ds(start, tk), p * LANES:(p + 1) * LANES]
            cum = cum_all[p * rows:(p + 1) * rows]
            att = jnp.exp2(zs[p] - cum - runs[p])
            if diag_offset is not None:
                att = jnp.where(valid, att, 0.0)
            new_accs.append(accs[p] + jnp.dot(att.astype(BF16), v, preferred_element_type=F32))
            new_runs.append(runs[p] + cum[:, 0:1])
        return tuple(new_runs), tuple(new_accs)

    assert tq & (tq - 1) == 0
    runs = tuple(jnp.zeros((rows, 1), F32) for _ in range(n_pair))
    accs = tuple(jnp.zeros((rows, LANES), F32) for _ in range(n_pair))
    for d in range(n_diag):
        dd = n_diag - 1 - d
        runs, accs = tile(qi * n_diag + dd, runs, accs, dd * tk)
    has_left = qi * n_diag >= 1
    runs, accs = tile(jnp.maximum(qi * n_diag - 1, 0), runs, accs, None,
                      bias=jnp.where(has_left, 0.0, ATT_OFF).astype(F32))
    kb0 = qi * n_diag - 2

    def cond(state):
        kb, runs, _ = state
        least = functools.reduce(jnp.minimum, runs)
        return (kb >= 0) & (jnp.min(least) < ATT_STOP)

    def body(state):
        kb, runs, accs = state
        runs, accs = tile(kb, runs, accs, None)
        return kb - 1, runs, accs

    _, _, accs = lax.while_loop(cond, body, (kb0, runs, accs))
    for p in range(n_pair):
        o_ref[:, p * LANES:(p + 1) * LANES] = jnp.where(
            lane < SB_HD, accs[p][:tq], accs[p][tq:]).astype(o_ref.dtype)


def _attention(qkv, umat, tq, pairs_per_step):
    L = qkv.shape[0]
    n_pairs = SB_HEADS * SB_HD // LANES
    n_steps = n_pairs // pairs_per_step
    width = pairs_per_step * LANES
    tk = umat.shape[0]
    return pl.pallas_call(
        _attn_kernel,
        grid=(n_steps, L // tq),
        in_specs=[
            pl.BlockSpec((tq, width), lambda hp, i: (i, hp)),
            pl.BlockSpec((L, width), lambda hp, i: (0, n_steps + hp), pipeline_mode=pl.Buffered(1)),
            pl.BlockSpec((L, width), lambda hp, i: (0, 2 * n_steps + hp), pipeline_mode=pl.Buffered(1)),
            pl.BlockSpec((tk, tk), lambda hp, i: (0, 0)),
        ],
        out_specs=pl.BlockSpec((tq, width), lambda hp, i: (i, hp)),
        out_shape=jax.ShapeDtypeStruct((L, SB_HEADS * SB_HD), BF16),
        compiler_params=_cparams("arbitrary", "arbitrary"),
        name="stickbreak_attention",
    )(qkv, qkv, qkv, umat)


def _row(v, width=None):
    v = v.astype(F32).reshape(1, -1)
    if width is not None and v.shape[1] < width:
        v = jnp.pad(v, ((0, 0), (0, width - v.shape[1])))
    return v


def _pad_rows(w, rows):
    return jnp.pad(w.astype(F32), ((0, rows - w.shape[0]), (0, 0)))


def _ffn_conv_weights(conv_w):
    return _pad_rows(conv_w, SUBLANES).reshape(SUBLANES, 2, N_FF_CHUNKS, FF_CHUNK).transpose(2, 1, 0, 3)


def kernel(x, mix0_norm, w_in0, a_dw_w, a_dw_b, a_ln_w, a_ln_b, gdn_conv_w, gdn_a_log,
           gdn_dt_bias, gdn_onorm_w, w_out0, mix1_norm, w_qkv1, w_out1,
           ffn_norm, w_up, ffn_conv_w, w_down, final_norm):
    B, L, D = x.shape
    assert B == 1 and D == D_MODEL
    xs = x.reshape(L, D)

    w_in = w_in0[0].astype(BF16)
    w_ab = jnp.pad(w_in[:, IN0_MAIN:], ((0, 0), (0, LANES - (w_in.shape[1] - IN0_MAIN))))
    p, ab = _rms_matmul(xs, _row(mix0_norm[0]), w_in, jnp.ones((1, IN0_MAIN), F32), BF16,
                        tm=PROJ_TM, tn=768, w_side=w_ab)
    wd = w_down.astype(BF16).reshape(w_down.shape[0], N_FF_CHUNKS, FF_CHUNK, D)
    wu = _chunk_up_weights(w_up, per_step=N_FF_CHUNKS)
    y_a = _conformer(p, _pad_rows(a_dw_w[0], CONF_HALO), _row(a_dw_b[0]), _row(a_ln_w[0]),
                     _row(a_ln_b[0]), tm=512)
    y_b = _gdn(p, ab, _pad_rows(gdn_conv_w[0], SUBLANES), _row(gdn_a_log[0], LANES),
               _row(gdn_dt_bias[0], LANES), _row(gdn_onorm_w[0]), tb=4 * GDN_CHUNK)
    cw = _ffn_conv_weights(ffn_conv_w[0])
    xs = _ffn(xs, y_a, 0, y_b, 0, w_out0[0].astype(BF16), _row(ffn_norm[0]), wu, 0, cw, wd,
              _row(final_norm), tm=FFN_TM, final_norm=False)

    hd = SB_HEADS * SB_HD
    qscale = jnp.concatenate([jnp.full((1, hd), SB_HD ** -0.5 * LOG2E, F32),
                              jnp.ones((1, 2 * hd), F32)], axis=1)
    qkv = _rms_matmul(xs, _row(mix1_norm[0]), w_qkv1[0].astype(BF16), qscale, BF16, tm=PROJ_TM, tn=768)
    ki = lax.broadcasted_iota(jnp.int32, (ATT_TK, ATT_TK), 0)
    si = lax.broadcasted_iota(jnp.int32, (ATT_TK, ATT_TK), 1)
    umat = (ki >= si).astype(BF16)
    o = _attention(qkv, umat, tq=ATT_TQ, pairs_per_step=ATT_PAIRS)
    cw = _ffn_conv_weights(ffn_conv_w[1])
    xs = _ffn(xs, o, 0, o, 1, w_out1[0].astype(BF16), _row(ffn_norm[1]), wu, 1, cw, wd,
              _row(final_norm), tm=FFN_TM, final_norm=True)
    return xs.reshape(B, L, D)
```

```python
import functools

import jax
import jax.numpy as jnp
from jax import lax
from jax.experimental import pallas as pl
from jax.experimental.pallas import tpu as pltpu

F32 = jnp.float32
BF16 = jnp.bfloat16

D_MODEL = 1024
RMS_EPS = 1e-6
LN_EPS = 1e-5
CONV_CH = 512
CONV_WIDTH = 31
GDN_HEADS = 4
GDN_DK = 128
GDN_KEY = GDN_HEADS * GDN_DK
SHORT_CONV = 4
SB_HEADS = 16
SB_HD = 64
D_FF = 2816
FFN_CONV = 3

LANES = 128
SUBLANES = 8
VMEM_LIMIT = 60 * 1024 * 1024

IN0_MAIN = 3072
FF_CHUNK = 256
N_FF_CHUNKS = D_FF // FF_CHUNK
FFN_TM = 1024
PROJ_TM = 1024
PROJ_TN = 768
GDN_CHUNK = 128
GDN_TB = 4 * GDN_CHUNK
CONF_TM = 512
CONF_HALO = 32
CONF_SUB = 32
ATT_TQ = 256
ATT_TK = 256
ATT_PAIRS = 4
LOG2E = 1.4426950408889634
ATT_STOP = 160.0
ATT_OFF = -1e30


def _cparams(*sem):
    return pltpu.CompilerParams(dimension_semantics=sem, vmem_limit_bytes=VMEM_LIMIT)


def _rms_matmul_kernel(*refs, has_side, tn):
    if has_side:
        x_ref, nw_ref, w_ref, cs_ref, ws_ref, o_ref, os_ref, h_ref = refs
    else:
        x_ref, nw_ref, w_ref, cs_ref, o_ref, h_ref = refs

    x = x_ref[...]
    ms = jnp.mean(x * x, axis=-1, keepdims=True)
    h_ref[...] = (x * lax.rsqrt(ms + RMS_EPS) * nw_ref[...]).astype(BF16)
    if has_side:
        os_ref[...] = jnp.dot(h_ref[...], ws_ref[...], preferred_element_type=F32)
    for n in range(o_ref.shape[1] // tn):
        cols = slice(n * tn, (n + 1) * tn)
        acc = jnp.dot(h_ref[...], w_ref[:, cols], preferred_element_type=F32)
        o_ref[:, cols] = (acc * cs_ref[:, cols]).astype(o_ref.dtype)


def _rms_matmul(x, nw, w, colscale, out_dtype, tm, tn, w_side=None):
    L, D = x.shape
    N = colscale.shape[1]
    has_side = w_side is not None
    const = lambda i: (0, 0)
    in_specs = [
        pl.BlockSpec((tm, D), lambda i: (i, 0)),
        pl.BlockSpec((1, D), const),
        pl.BlockSpec((D, N), const, pipeline_mode=pl.Buffered(1)),
        pl.BlockSpec((1, N), const),
    ]
    out_specs = [pl.BlockSpec((tm, N), lambda i: (i, 0))]
    out_shape = [jax.ShapeDtypeStruct((L, N), out_dtype)]
    args = [x, nw, w, colscale]
    if has_side:
        ns = w_side.shape[1]
        in_specs.append(pl.BlockSpec((D, ns), const))
        out_specs.append(pl.BlockSpec((tm, ns), lambda i: (i, 0)))
        out_shape.append(jax.ShapeDtypeStruct((L, ns), F32))
        args.append(w_side)
    outs = pl.pallas_call(
        functools.partial(_rms_matmul_kernel, has_side=has_side, tn=tn),
        grid=(L // tm,),
        in_specs=in_specs,
        out_specs=out_specs,
        out_shape=out_shape,
        scratch_shapes=[pltpu.VMEM((tm, D), BF16)],
        compiler_params=_cparams("arbitrary"),
        name="rms_matmul",
    )(*args)
    return outs if has_side else outs[0]


def _conformer_kernel(val_ref, gate_ref, w_ref, b_ref, lnw_ref, lnb_ref, o_ref, ext_ref, sh_ref):
    tm = val_ref.shape[0]
    i = pl.program_id(0)

    @pl.when(i == 0)
    def _():
        ext_ref[0:CONF_HALO, :] = jnp.zeros((CONF_HALO, CONV_CH), F32)

    @pl.when(i > 0)
    def _():
        ext_ref[0:CONF_HALO, :] = ext_ref[tm:tm + CONF_HALO, :]

    ext_ref[CONF_HALO:, :] = val_ref[...].astype(F32) * jax.nn.sigmoid(gate_ref[...].astype(F32))

    nrows = tm + CONF_HALO - SUBLANES
    for b in range(1, SUBLANES):
        sh_ref[b, 0:nrows, :] = ext_ref[pl.ds(b, nrows), :]

    first = CONF_HALO - (CONV_WIDTH - 1)

    def body(r, carry):
        base = pl.multiple_of(r * CONF_SUB, CONF_SUB)
        acc = jnp.zeros((CONF_SUB, CONV_CH), F32)
        for k in range(CONV_WIDTH):
            a, b = divmod(first + k, SUBLANES)
            src = ext_ref if b == 0 else sh_ref.at[b]
            acc = acc + src[pl.ds(base + SUBLANES * a, CONF_SUB), :] * w_ref[k:k + 1, :]
        acc = acc + b_ref[...]
        mu = jnp.mean(acc, axis=-1, keepdims=True)
        xc = acc - mu
        var = jnp.mean(xc * xc, axis=-1, keepdims=True)
        y = xc * lax.rsqrt(var + LN_EPS) * lnw_ref[...] + lnb_ref[...]
        o_ref[pl.ds(base, CONF_SUB), :] = (y * jax.nn.sigmoid(y)).astype(o_ref.dtype)
        return carry

    lax.fori_loop(0, tm // CONF_SUB, body, 0, unroll=8)


def _conformer(p, dw_w, dw_b, ln_w, ln_b, tm):
    L = p.shape[0]
    row = lambda i: (0, 0)
    return pl.pallas_call(
        _conformer_kernel,
        grid=(L // tm,),
        in_specs=[
            pl.BlockSpec((tm, CONV_CH), lambda i: (i, 0)),
            pl.BlockSpec((tm, CONV_CH), lambda i: (i, 1)),
            pl.BlockSpec((CONF_HALO, CONV_CH), row),
            pl.BlockSpec((1, CONV_CH), row),
            pl.BlockSpec((1, CONV_CH), row),
            pl.BlockSpec((1, CONV_CH), row),
        ],
        out_specs=pl.BlockSpec((tm, CONV_CH), lambda i: (i, 0)),
        out_shape=jax.ShapeDtypeStruct((L, CONV_CH), BF16),
        scratch_shapes=[pltpu.VMEM((tm + CONF_HALO, CONV_CH), F32),
                        pltpu.VMEM((SUBLANES, tm + CONF_HALO, CONV_CH), F32)],
        compiler_params=_cparams("arbitrary"),
        name="conformer_conv",
    )(p, p, dw_w, dw_b, ln_w, ln_b)


def _bmm(a, b):
    return lax.dot_general(a.astype(BF16), b.astype(BF16), (((2,), (1,)), ((0,), (0,))),
                           preferred_element_type=F32)


def _bmm_nt(a, b):
    return lax.dot_general(a.astype(BF16), b.astype(BF16), (((2,), (2,)), ((0,), (0,))),
                           preferred_element_type=F32)


def _unit_lower_inverse(low, row, col):
    n = low.shape[-1]
    eye = (row == col).astype(F32)

    def same_block(b):
        sh = b.bit_length() - 1
        return jnp.right_shift(row, sh) == jnp.right_shift(col, sh)

    base = SUBLANES
    a = jnp.where(same_block(base), low, 0.0)
    a2 = _bmm(a, a)
    a4 = _bmm(a2, a2)
    ia = eye - a
    p1 = ia + _bmm(ia, a2)
    t = p1 + _bmm(p1, a4)
    b = base
    while b < n:
        c = jnp.where(same_block(2 * b) & jnp.logical_not(same_block(b)), low, 0.0)
        t = t - _bmm(_bmm(t, c), t)
        b *= 2
    return t


def _gdn_kernel(q_ref, k_ref, v_ref, z_ref, ab_ref, cw_ref, alog_ref, dtb_ref, onw_ref,
                o_ref, ext_ref, s_ref, qkv_ref, gc_ref, gct_ref, beta_ref):
    tb = q_ref.shape[0]
    nc = tb // GDN_CHUNK
    c_ = GDN_CHUNK
    i = pl.program_id(0)

    @pl.when(i == 0)
    def _():
        ext_ref[0:SUBLANES, :] = jnp.zeros((SUBLANES, 3 * GDN_KEY), F32)
        s_ref[...] = jnp.zeros(s_ref.shape, F32)

    @pl.when(i > 0)
    def _():
        ext_ref[0:SUBLANES, :] = ext_ref[tb:tb + SUBLANES, :]

    ext_ref[SUBLANES:, 0:GDN_KEY] = q_ref[...].astype(F32)
    ext_ref[SUBLANES:, GDN_KEY:2 * GDN_KEY] = k_ref[...].astype(F32)
    ext_ref[SUBLANES:, 2 * GDN_KEY:] = v_ref[...].astype(F32)

    first = SUBLANES - (SHORT_CONV - 1)
    conv = ext_ref[pl.ds(first, tb), :] * cw_ref[0:1, :]
    for t in range(1, SHORT_CONV):
        conv = conv + ext_ref[pl.ds(first + t, tb), :] * cw_ref[t:t + 1, :]
    qkv = conv * jax.nn.sigmoid(conv)

    ab = ab_ref[...]
    g = -jnp.exp(alog_ref[...]) * jax.nn.softplus(ab + dtb_ref[...])
    beta = jax.nn.sigmoid(ab)

    rr = lax.broadcasted_iota(jnp.int32, (tb, tb), 0)
    cc = lax.broadcasted_iota(jnp.int32, (tb, tb), 1)
    csh = c_.bit_length() - 1
    tri = ((rr >= cc) & (jnp.right_shift(rr, csh) == jnp.right_shift(cc, csh))).astype(BF16)
    g_hi = g.astype(BF16)
    g_r1 = g - g_hi.astype(F32)
    g_mid = g_r1.astype(BF16)
    g_lo = (g_r1 - g_mid.astype(F32)).astype(BF16)
    gc = (jnp.dot(tri, g_hi, preferred_element_type=F32)
          + jnp.dot(tri, g_mid, preferred_element_type=F32)
          + jnp.dot(tri, g_lo, preferred_element_type=F32))
    qkv_ref[...] = qkv
    gc_ref[...] = gc
    gct_ref[...] = gc.T
    beta_ref[...] = beta

    row = lax.broadcasted_iota(jnp.int32, (c_, c_), 0)
    col = lax.broadcasted_iota(jnp.int32, (c_, c_), 1)
    incl = row >= col
    strict = row > col

    pairs = [(c, h) for c in range(nc) for h in range(GDN_HEADS)]

    def stack(fn):
        return jnp.stack([fn(c * c_, h) for c, h in pairs])

    q = stack(lambda r0, h: qkv_ref[r0:r0 + c_, h * GDN_DK:(h + 1) * GDN_DK])
    k = stack(lambda r0, h: qkv_ref[r0:r0 + c_, GDN_KEY + h * GDN_DK:GDN_KEY + (h + 1) * GDN_DK])
    v = stack(lambda r0, h: qkv_ref[r0:r0 + c_, 2 * GDN_KEY + h * GDN_DK:2 * GDN_KEY + (h + 1) * GDN_DK])
    gcol = stack(lambda r0, h: gc_ref[r0:r0 + c_, h:h + 1])
    grow = stack(lambda r0, h: gct_ref[h:h + 1, r0:r0 + c_])
    bcol = stack(lambda r0, h: beta_ref[r0:r0 + c_, GDN_HEADS + h:GDN_HEADS + h + 1])
    glast = gcol[:, c_ - 1:c_, :]

    q = q * lax.rsqrt(jnp.sum(q * q, axis=-1, keepdims=True) + RMS_EPS) * (GDN_DK ** -0.5)
    k = k * lax.rsqrt(jnp.sum(k * k, axis=-1, keepdims=True) + RMS_EPS)
    decay = jnp.where(incl, jnp.exp(gcol - grow), 0.0)
    egc = jnp.exp(gcol)
    kb = k * bcol
    low = jnp.where(strict, _bmm_nt(kb, k) * decay, 0.0)
    tinv = _unit_lower_inverse(low, row, col)
    uw = _bmm(tinv, jnp.concatenate([v * bcol, kb * egc], axis=-1))
    qk = _bmm_nt(q, k) * decay
    ktt = jnp.swapaxes(k * jnp.exp(glast - gcol), 1, 2)
    lhs_state = jnp.concatenate([uw[:, :, GDN_DK:], q * egc], axis=1)
    lhs_vnew = jnp.concatenate([qk, ktt], axis=1)
    u = uw[:, :, :GDN_DK]
    sdecay = jnp.exp(glast)

    s = s_ref[...]
    for c in range(nc):
        sl = slice(c * GDN_HEADS, (c + 1) * GDN_HEADS)
        from_state = _bmm(lhs_state[sl], s)
        v_new = u[sl] - from_state[:, :c_]
        from_vnew = _bmm(lhs_vnew[sl], v_new)
        o = from_state[:, c_:] + from_vnew[:, :c_]
        s = s * sdecay[sl] + from_vnew[:, c_:]
        on = o * lax.rsqrt(jnp.mean(o * o, axis=-1, keepdims=True) + RMS_EPS) * onw_ref[...]
        r0 = c * c_
        for h in range(GDN_HEADS):
            zh = z_ref[r0:r0 + c_, h * GDN_DK:(h + 1) * GDN_DK].astype(F32)
            o_ref[r0:r0 + c_, h * GDN_DK:(h + 1) * GDN_DK] = (
                on[h] * (zh * jax.nn.sigmoid(zh))).astype(o_ref.dtype)
    s_ref[...] = s


def _gdn(p, ab, conv_w, alog_row, dtb_row, onorm_row, tb):
    L = p.shape[0]
    row = lambda i: (0, 0)
    return pl.pallas_call(
        _gdn_kernel,
        grid=(L // tb,),
        in_specs=[
            pl.BlockSpec((tb, GDN_KEY), lambda i: (i, 2)),
            pl.BlockSpec((tb, GDN_KEY), lambda i: (i, 3)),
            pl.BlockSpec((tb, GDN_KEY), lambda i: (i, 4)),
            pl.BlockSpec((tb, GDN_KEY), lambda i: (i, 5)),
            pl.BlockSpec((tb, LANES), lambda i: (i, 0)),
            pl.BlockSpec((SUBLANES, 3 * GDN_KEY), row),
            pl.BlockSpec((1, LANES), row),
            pl.BlockSpec((1, LANES), row),
            pl.BlockSpec((1, LANES), row),
        ],
        out_specs=pl.BlockSpec((tb, GDN_KEY), lambda i: (i, 0)),
        out_shape=jax.ShapeDtypeStruct((L, GDN_KEY), BF16),
        scratch_shapes=[pltpu.VMEM((tb + SUBLANES, 3 * GDN_KEY), F32),
                        pltpu.VMEM((GDN_HEADS, GDN_DK, GDN_DK), F32),
                        pltpu.VMEM((tb, 3 * GDN_KEY), F32),
                        pltpu.VMEM((tb, LANES), F32),
                        pltpu.VMEM((LANES, tb), F32),
                        pltpu.VMEM((tb, LANES), F32)],
        compiler_params=_cparams("arbitrary"),
        name="gated_deltanet",
    )(p, p, p, p, ab, conv_w, alog_row, dtb_row, onorm_row)


def _ffn_kernel(x_ref, ya_ref, yb_ref, wo_ref, nw_ref, wg_ref, wv_ref, cw_ref, wd_ref, fnw_ref,
                o_ref, h_ref, acc_ref, ubuf_a, ubuf_b, carry_ref, *, final_norm):
    tm = x_ref.shape[0]
    kh = ya_ref.shape[1]
    nv = tm // SUBLANES
    pitch = nv + SUBLANES
    halo = (FFN_CONV - 1) * SUBLANES
    i = pl.program_id(0)

    @pl.when(i == 0)
    def _():
        carry_ref[...] = jnp.zeros(carry_ref.shape, F32)

    x = (x_ref[...] + jnp.dot(ya_ref[...], wo_ref[0:kh, :], preferred_element_type=F32)
         + jnp.dot(yb_ref[...], wo_ref[kh:, :], preferred_element_type=F32))
    o_ref[...] = x
    ms = jnp.mean(x * x, axis=-1, keepdims=True)
    hn = x * lax.rsqrt(ms + RMS_EPS) * nw_ref[...]
    n_slab = acc_ref.shape[0]
    for s in range(SUBLANES):
        for l in range(n_slab):
            acc_ref[l, pitch * s:pitch * s + nv, :] = hn[nv * s:nv * (s + 1), l * LANES:(l + 1) * LANES]
    for v in range(0, nv, 2):
        for l in range(n_slab):
            pair = jnp.concatenate([acc_ref[l, pl.ds(v, SUBLANES, stride=pitch), :],
                                    acc_ref[l, pl.ds(v + 1, SUBLANES, stride=pitch), :]], axis=0)
            h_ref[SUBLANES * v:SUBLANES * (v + 2), l * LANES:(l + 1) * LANES] = pair.astype(BF16)
    acc_ref[:, 0:tm, :] = jnp.zeros((n_slab, tm, LANES), F32)
    sub = lax.broadcasted_iota(jnp.int32, (SUBLANES, FF_CHUNK), 0)

    def up(c, ubuf):
        h = h_ref[...]
        for gv, w_ref in enumerate((wg_ref, wv_ref)):
            u = jnp.dot(h, w_ref[c], preferred_element_type=F32)
            prev = carry_ref[c, gv]
            for g in range(FFN_CONV - 1):
                lo = tm - halo + SUBLANES * g
                wrapped = pltpu.roll(u[lo:lo + SUBLANES, :], 1, 0)
                top = prev[SUBLANES * g + SUBLANES - 1:SUBLANES * (g + 1), :]
                ubuf[gv, SUBLANES * g:SUBLANES * (g + 1), :] = jnp.where(sub == 0, top, wrapped)
            ubuf[gv, halo:, :] = u
            carry_ref[c, gv] = u[tm - halo:, :]

    def down(c, ubuf):
        cw = cw_ref[c]
        gate, val = [
            functools.reduce(jnp.add, [
                ubuf[gv, pl.ds(SUBLANES * t, tm), :] * cw[gv, t:t + 1, :]
                for t in range(FFN_CONV)])
            for gv in range(2)]
        act = (gate * jax.nn.sigmoid(gate) * val).astype(BF16)
        d = jnp.dot(act, wd_ref[c], preferred_element_type=F32)
        for l in range(n_slab):
            acc_ref[l, 0:tm, :] += d[:, l * LANES:(l + 1) * LANES]

    ubufs = (ubuf_a, ubuf_b)
    up(0, ubufs[0])
    for c in range(N_FF_CHUNKS):
        if c + 1 < N_FF_CHUNKS:
            up(c + 1, ubufs[(c + 1) % 2])
        down(c, ubufs[c % 2])
    runs = nv // SUBLANES
    for j in range(nv):
        first = SUBLANES * SUBLANES * (j % runs) + j // runs
        rows = slice(SUBLANES * j, SUBLANES * (j + 1))
        y = o_ref[rows, :] + jnp.concatenate(
            [acc_ref[l, pl.ds(first, SUBLANES, stride=SUBLANES), :] for l in range(n_slab)], axis=1)
        if final_norm:
            ms2 = jnp.mean(y * y, axis=-1, keepdims=True)
            y = y * lax.rsqrt(ms2 + RMS_EPS) * fnw_ref[...]
        o_ref[rows, :] = y


def _ffn(x, ya, ca, yb, cb, w_out, nw, wu, layer, cw, wd, fnw, tm, final_norm):
    L, D = x.shape
    kh = w_out.shape[0] // 2
    halo = (FFN_CONV - 1) * SUBLANES
    assert tm % (SUBLANES * SUBLANES) == 0
    const2 = lambda i: (0, 0)
    resident = dict(pipeline_mode=pl.Buffered(1))
    return pl.pallas_call(
        functools.partial(_ffn_kernel, final_norm=final_norm),
        grid=(L // tm,),
        in_specs=[
            pl.BlockSpec((tm, D), lambda i: (i, 0)),
            pl.BlockSpec((tm, kh), lambda i: (i, ca)),
            pl.BlockSpec((tm, kh), lambda i: (i, cb)),
            pl.BlockSpec((2 * kh, D), const2, **resident),
            pl.BlockSpec((1, D), const2),
            pl.BlockSpec((None, N_FF_CHUNKS, D, FF_CHUNK), lambda i: (layer, 0, 0, 0), **resident),
            pl.BlockSpec((None, N_FF_CHUNKS, D, FF_CHUNK), lambda i: (layer, 1, 0, 0), **resident),
            pl.BlockSpec((N_FF_CHUNKS, 2, SUBLANES, FF_CHUNK), lambda i: (0, 0, 0, 0)),
            pl.BlockSpec((None, N_FF_CHUNKS, FF_CHUNK, D), lambda i: (layer, 0, 0, 0), **resident),
            pl.BlockSpec((1, D), const2),
        ],
        out_specs=pl.BlockSpec((tm, D), lambda i: (i, 0)),
        out_shape=jax.ShapeDtypeStruct((L, D), F32),
        scratch_shapes=[pltpu.VMEM((tm, D), BF16),
                        pltpu.VMEM((D // LANES, tm + SUBLANES * SUBLANES, LANES), F32),
                        pltpu.VMEM((2, tm + halo, FF_CHUNK), F32),
                        pltpu.VMEM((2, tm + halo, FF_CHUNK), F32),
                        pltpu.VMEM((N_FF_CHUNKS, 2, halo, FF_CHUNK), F32)],
        compiler_params=_cparams("arbitrary"),
        name="conv_ffn",
    )(x, ya, yb, w_out, nw, wu, wu, cw, wd, fnw)


def _chunk_weights_kernel(w_ref, o_ref):
    for g in range(o_ref.shape[1]):
        o_ref[0, g] = w_ref[0, :, g * FF_CHUNK:(g + 1) * FF_CHUNK].astype(o_ref.dtype)


def _chunk_up_weights(w_up, per_step=2):
    nl, d, n = w_up.shape
    nchunk = n // FF_CHUNK
    return pl.pallas_call(
        _chunk_weights_kernel,
        grid=(nl, nchunk // per_step),
        in_specs=[pl.BlockSpec((1, d, per_step * FF_CHUNK), lambda l, c: (l, 0, c))],
        out_specs=pl.BlockSpec((1, per_step, d, FF_CHUNK), lambda l, c: (l, c, 0, 0)),
        out_shape=jax.ShapeDtypeStruct((nl, nchunk, d, FF_CHUNK), BF16),
        compiler_params=_cparams("arbitrary", "arbitrary"),
        name="chunk_up_weights",
    )(w_up)


def _softplus2(z):
    return jnp.maximum(jnp.log2(1.0 + jnp.exp2(jnp.minimum(z, 126.0))), z)


def _attn_kernel(q_ref, k_ref, v_ref, u_ref, o_ref):
    tq = q_ref.shape[0]
    tk = u_ref.shape[0]
    qi = pl.program_id(1)
    n_diag = tq // tk
    n_pair = q_ref.shape[1] // LANES
    lane = lax.broadcasted_iota(jnp.int32, (1, LANES), 1)
    umat = u_ref[...]
    rows = 2 * tq
    rowq = jnp.bitwise_and(lax.broadcasted_iota(jnp.int32, (rows, tk), 0), tq - 1)
    colk = lax.broadcasted_iota(jnp.int32, (rows, tk), 1)

    qs = []
    for p in range(n_pair):
        q = q_ref[:, p * LANES:(p + 1) * LANES]
        zero = jnp.zeros_like(q)
        qs.append(jnp.concatenate([jnp.where(lane < SB_HD, q, zero),
                                   jnp.where(lane >= SB_HD, q, zero)], axis=0))

    def tile(kb, runs, accs, diag_offset, bias=None):
        start = pl.multiple_of(kb * tk, tk)
        zs, sps = [], []
        for p in range(n_pair):
            k = k_ref[pl.ds(start, tk), p * LANES:(p + 1) * LANES]
            z = lax.dot_general(qs[p], k, (((1,), (1,)), ((), ())), preferred_element_type=F32)
            if bias is not None:
                z = z + bias
            sp = _softplus2(z)
            if diag_offset is not None:
                valid = (colk + diag_offset) < rowq
                sp = jnp.where(valid, sp, 0.0)
            zs.append(z)
            sps.append(sp.astype(BF16))
        cum_all = jnp.dot(jnp.concatenate(sps, axis=0), umat, preferred_element_type=F32)
        new_runs, new_accs = [], []
        for p in range(n_pair):
            v = v_ref[pl.ds(start, tk), p * LANES:(p + 1) * LANES]
            cum = cum_all[p * rows:(p + 1) * rows]
            att = jnp.exp2(zs[p] - cum - runs[p])
            if diag_offset is not None:
                att = jnp.where(valid, att, 0.0)
            new_accs.append(accs[p] + jnp.dot(att.astype(BF16), v, preferred_element_type=F32))
            new_runs.append(runs[p] + cum[:, 0:1])
        return tuple(new_runs), tuple(new_accs)

    assert tq & (tq - 1) == 0
    runs = tuple(jnp.zeros((rows, 1), F32) for _ in range(n_pair))
    accs = tuple(jnp.zeros((rows, LANES), F32) for _ in range(n_pair))
    for d in range(n_diag):
        dd = n_diag - 1 - d
        runs, accs = tile(qi * n_diag + dd, runs, accs, dd * tk)
    has_left = qi * n_diag >= 1
    runs, accs = tile(jnp.maximum(qi * n_diag - 1, 0), runs, accs, None,
                      bias=jnp.where(has_left, 0.0, ATT_OFF).astype(F32))
    kb0 = qi * n_diag - 2

    def cond(state):
        kb, runs, _ = state
        least = functools.reduce(jnp.minimum, runs)
        return (kb >= 0) & (jnp.min(least) < ATT_STOP)

    def body(state):
        kb, runs, accs = state
        runs, accs = tile(kb, runs, accs, None)
        return kb - 1, runs, accs

    _, _, accs = lax.while_loop(cond, body, (kb0, runs, accs))
    for p in range(n_pair):
        o_ref[:, p * LANES:(p + 1) * LANES] = jnp.where(
            lane < SB_HD, accs[p][:tq], accs[p][tq:]).astype(o_ref.dtype)


def _attention(qkv, umat, tq, pairs_per_step):
    L = qkv.shape[0]
    n_pairs = SB_HEADS * SB_HD // LANES
    n_steps = n_pairs // pairs_per_step
    width = pairs_per_step * LANES
    tk = umat.shape[0]
    return pl.pallas_call(
        _attn_kernel,
        grid=(n_steps, L // tq),
        in_specs=[
            pl.BlockSpec((tq, width), lambda hp, i: (i, hp)),
            pl.BlockSpec((L, width), lambda hp, i: (0, n_steps + hp), pipeline_mode=pl.Buffered(1)),
            pl.BlockSpec((L, width), lambda hp, i: (0, 2 * n_steps + hp), pipeline_mode=pl.Buffered(1)),
            pl.BlockSpec((tk, tk), lambda hp, i: (0, 0)),
        ],
        out_specs=pl.BlockSpec((tq, width), lambda hp, i: (i, hp)),
        out_shape=jax.ShapeDtypeStruct((L, SB_HEADS * SB_HD), BF16),
        compiler_params=_cparams("arbitrary", "arbitrary"),
        name="stickbreak_attention",
    )(qkv, qkv, qkv, umat)


def _row(v, width=None):
    v = v.astype(F32).reshape(1, -1)
    if width is not None and v.shape[1] < width:
        v = jnp.pad(v, ((0, 0), (0, width - v.shape[1])))
    return v


def _pad_rows(w, rows):
    return jnp.pad(w.astype(F32), ((0, rows - w.shape[0]), (0, 0)))


def _ffn_conv_weights(conv_w):
    return _pad_rows(conv_w, SUBLANES).reshape(SUBLANES, 2, N_FF_CHUNKS, FF_CHUNK).transpose(2, 1, 0, 3)


def kernel(x, mix0_norm, w_in0, a_dw_w, a_dw_b, a_ln_w, a_ln_b, gdn_conv_w, gdn_a_log,
           gdn_dt_bias, gdn_onorm_w, w_out0, mix1_norm, w_qkv1, w_out1,
           ffn_norm, w_up, ffn_conv_w, w_down, final_norm):
    B, L, D = x.shape
    assert B == 1 and D == D_MODEL
    xs = x.reshape(L, D)

    w_in = w_in0[0].astype(BF16)
    w_ab = jnp.pad(w_in[:, IN0_MAIN:], ((0, 0), (0, LANES - (w_in.shape[1] - IN0_MAIN))))
    p, ab = _rms_matmul(xs, _row(mix0_norm[0]), w_in, jnp.ones((1, IN0_MAIN), F32), BF16,
                        tm=PROJ_TM, tn=PROJ_TN, w_side=w_ab)
    wd = w_down.astype(BF16).reshape(w_down.shape[0], N_FF_CHUNKS, FF_CHUNK, D)
    wu = _chunk_up_weights(w_up, per_step=N_FF_CHUNKS)
    y_a = _conformer(p, _pad_rows(a_dw_w[0], CONF_HALO), _row(a_dw_b[0]), _row(a_ln_w[0]),
                     _row(a_ln_b[0]), tm=CONF_TM)
    y_b = _gdn(p, ab, _pad_rows(gdn_conv_w[0], SUBLANES), _row(gdn_a_log[0], LANES),
               _row(gdn_dt_bias[0], LANES), _row(gdn_onorm_w[0]), tb=GDN_TB)
    cw = _ffn_conv_weights(ffn_conv_w[0])
    xs = _ffn(xs, y_a, 0, y_b, 0, w_out0[0].astype(BF16), _row(ffn_norm[0]), wu, 0, cw, wd,
              _row(final_norm), tm=FFN_TM, final_norm=False)

    hd = SB_HEADS * SB_HD
    qscale = jnp.concatenate([jnp.full((1, hd), SB_HD ** -0.5 * LOG2E, F32),
                              jnp.ones((1, 2 * hd), F32)], axis=1)
    qkv = _rms_matmul(xs, _row(mix1_norm[0]), w_qkv1[0].astype(BF16), qscale, BF16, tm=PROJ_TM,
                      tn=PROJ_TN)
    ki = lax.broadcasted_iota(jnp.int32, (ATT_TK, ATT_TK), 0)
    si = lax.broadcasted_iota(jnp.int32, (ATT_TK, ATT_TK), 1)
    umat = (ki >= si).astype(BF16)
    o = _attention(qkv, umat, tq=ATT_TQ, pairs_per_step=ATT_PAIRS)
    cw = _ffn_conv_weights(ffn_conv_w[1])
    xs = _ffn(xs, o, 0, o, 1, w_out1[0].astype(BF16), _row(ffn_norm[1]), wu, 1, cw, wd,
              _row(final_norm), tm=FFN_TM, final_norm=True)
    return xs.reshape(B, L, D)
```

```python
import functools

import jax
import jax.numpy as jnp
from jax import lax
from jax.experimental import pallas as pl
from jax.experimental.pallas import tpu as pltpu

F32 = jnp.float32
BF16 = jnp.bfloat16

D_MODEL = 1024
RMS_EPS = 1e-6
LN_EPS = 1e-5
CONV_CH = 512
CONV_WIDTH = 31
GDN_HEADS = 4
GDN_DK = 128
GDN_KEY = GDN_HEADS * GDN_DK
SHORT_CONV = 4
SB_HEADS = 16
SB_HD = 64
D_FF = 2816
FFN_CONV = 3

LANES = 128
SUBLANES = 8
VMEM_LIMIT = 60 * 1024 * 1024

IN0_MAIN = 3072
FF_CHUNK = 256
N_FF_CHUNKS = D_FF // FF_CHUNK
FFN_TM = 1024
PROJ_TM = 1024
PROJ_TN = 768
GDN_CHUNK = 128
GDN_TB = 4 * GDN_CHUNK
CONF_TM = 512
CONF_HALO = 32
CONF_SUB = 64
CONF_UNROLL = 4
ATT_TQ = 256
ATT_TK = 256
ATT_PAIRS = 4
LOG2E = 1.4426950408889634
ATT_STOP = 160.0
ATT_OFF = -1e30


def _cparams(*sem):
    return pltpu.CompilerParams(dimension_semantics=sem, vmem_limit_bytes=VMEM_LIMIT)


def _rms_matmul_kernel(*refs, has_side, tn):
    if has_side:
        x_ref, nw_ref, w_ref, cs_ref, ws_ref, o_ref, os_ref, h_ref = refs
    else:
        x_ref, nw_ref, w_ref, cs_ref, o_ref, h_ref = refs

    x = x_ref[...]
    ms = jnp.mean(x * x, axis=-1, keepdims=True)
    h_ref[...] = (x * lax.rsqrt(ms + RMS_EPS) * nw_ref[...]).astype(BF16)
    if has_side:
        os_ref[...] = jnp.dot(h_ref[...], ws_ref[...], preferred_element_type=F32)
    for n in range(o_ref.shape[1] // tn):
        cols = slice(n * tn, (n + 1) * tn)
        acc = jnp.dot(h_ref[...], w_ref[:, cols], preferred_element_type=F32)
        o_ref[:, cols] = (acc * cs_ref[:, cols]).astype(o_ref.dtype)


def _rms_matmul(x, nw, w, colscale, out_dtype, tm, tn, w_side=None):
    L, D = x.shape
    N = colscale.shape[1]
    has_side = w_side is not None
    const = lambda i: (0, 0)
    in_specs = [
        pl.BlockSpec((tm, D), lambda i: (i, 0)),
        pl.BlockSpec((1, D), const),
        pl.BlockSpec((D, N), const, pipeline_mode=pl.Buffered(1)),
        pl.BlockSpec((1, N), const),
    ]
    out_specs = [pl.BlockSpec((tm, N), lambda i: (i, 0))]
    out_shape = [jax.ShapeDtypeStruct((L, N), out_dtype)]
    args = [x, nw, w, colscale]
    if has_side:
        ns = w_side.shape[1]
        in_specs.append(pl.BlockSpec((D, ns), const))
        out_specs.append(pl.BlockSpec((tm, ns), lambda i: (i, 0)))
        out_shape.append(jax.ShapeDtypeStruct((L, ns), F32))
        args.append(w_side)
    outs = pl.pallas_call(
        functools.partial(_rms_matmul_kernel, has_side=has_side, tn=tn),
        grid=(L // tm,),
        in_specs=in_specs,
        out_specs=out_specs,
        out_shape=out_shape,
        scratch_shapes=[pltpu.VMEM((tm, D), BF16)],
        compiler_params=_cparams("arbitrary"),
        name="rms_matmul",
    )(*args)
    return outs if has_side else outs[0]


def _conformer_kernel(val_ref, gate_ref, w_ref, b_ref, lnw_ref, lnb_ref, o_ref, ext_ref, sh_ref):
    tm = val_ref.shape[0]
    i = pl.program_id(0)

    @pl.when(i == 0)
    def _():
        ext_ref[0:CONF_HALO, :] = jnp.zeros((CONF_HALO, CONV_CH), F32)

    @pl.when(i > 0)
    def _():
        ext_ref[0:CONF_HALO, :] = ext_ref[tm:tm + CONF_HALO, :]

    ext_ref[CONF_HALO:, :] = val_ref[...].astype(F32) * jax.nn.sigmoid(gate_ref[...].astype(F32))

    nrows = tm + CONF_HALO - SUBLANES
    for b in range(1, SUBLANES):
        sh_ref[b, 0:nrows, :] = ext_ref[pl.ds(b, nrows), :]

    first = CONF_HALO - (CONV_WIDTH - 1)

    def body(r, carry):
        base = pl.multiple_of(r * CONF_SUB, CONF_SUB)
        acc = jnp.zeros((CONF_SUB, CONV_CH), F32)
        for k in range(CONV_WIDTH):
            a, b = divmod(first + k, SUBLANES)
            src = ext_ref if b == 0 else sh_ref.at[b]
            acc = acc + src[pl.ds(base + SUBLANES * a, CONF_SUB), :] * w_ref[k:k + 1, :]
        acc = acc + b_ref[...]
        mu = jnp.mean(acc, axis=-1, keepdims=True)
        xc = acc - mu
        var = jnp.mean(xc * xc, axis=-1, keepdims=True)
        y = xc * lax.rsqrt(var + LN_EPS) * lnw_ref[...] + lnb_ref[...]
        o_ref[pl.ds(base, CONF_SUB), :] = (y * jax.nn.sigmoid(y)).astype(o_ref.dtype)
        return carry

    lax.fori_loop(0, tm // CONF_SUB, body, 0, unroll=CONF_UNROLL)


def _conformer(p, dw_w, dw_b, ln_w, ln_b, tm):
    L = p.shape[0]
    row = lambda i: (0, 0)
    return pl.pallas_call(
        _conformer_kernel,
        grid=(L // tm,),
        in_specs=[
            pl.BlockSpec((tm, CONV_CH), lambda i: (i, 0)),
            pl.BlockSpec((tm, CONV_CH), lambda i: (i, 1)),
            pl.BlockSpec((CONF_HALO, CONV_CH), row),
            pl.BlockSpec((1, CONV_CH), row),
            pl.BlockSpec((1, CONV_CH), row),
            pl.BlockSpec((1, CONV_CH), row),
        ],
        out_specs=pl.BlockSpec((tm, CONV_CH), lambda i: (i, 0)),
        out_shape=jax.ShapeDtypeStruct((L, CONV_CH), BF16),
        scratch_shapes=[pltpu.VMEM((tm + CONF_HALO, CONV_CH), F32),
                        pltpu.VMEM((SUBLANES, tm + CONF_HALO, CONV_CH), F32)],
        compiler_params=_cparams("arbitrary"),
        name="conformer_conv",
    )(p, p, dw_w, dw_b, ln_w, ln_b)


def _bmm(a, b):
    return lax.dot_general(a.astype(BF16), b.astype(BF16), (((2,), (1,)), ((0,), (0,))),
                           preferred_element_type=F32)


def _bmm_nt(a, b):
    return lax.dot_general(a.astype(BF16), b.astype(BF16), (((2,), (2,)), ((0,), (0,))),
                           preferred_element_type=F32)


def _unit_lower_inverse(low, row, col):
    n = low.shape[-1]
    eye = (row == col).astype(F32)

    def same_block(b):
        sh = b.bit_length() - 1
        return jnp.right_shift(row, sh) == jnp.right_shift(col, sh)

    base = SUBLANES
    a = jnp.where(same_block(base), low, 0.0)
    a2 = _bmm(a, a)
    a4 = _bmm(a2, a2)
    ia = eye - a
    p1 = ia + _bmm(ia, a2)
    t = p1 + _bmm(p1, a4)
    b = base
    while b < n:
        c = jnp.where(same_block(2 * b) & jnp.logical_not(same_block(b)), low, 0.0)
        t = t - _bmm(_bmm(t, c), t)
        b *= 2
    return t


def _gdn_kernel(q_ref, k_ref, v_ref, z_ref, ab_ref, cw_ref, alog_ref, dtb_ref, onw_ref,
                o_ref, ext_ref, s_ref, qkv_ref, gc_ref, gct_ref, beta_ref):
    tb = q_ref.shape[0]
    nc = tb // GDN_CHUNK
    c_ = GDN_CHUNK
    i = pl.program_id(0)

    @pl.when(i == 0)
    def _():
        ext_ref[0:SUBLANES, :] = jnp.zeros((SUBLANES, 3 * GDN_KEY), F32)
        s_ref[...] = jnp.zeros(s_ref.shape, F32)

    @pl.when(i > 0)
    def _():
        ext_ref[0:SUBLANES, :] = ext_ref[tb:tb + SUBLANES, :]

    ext_ref[SUBLANES:, 0:GDN_KEY] = q_ref[...].astype(F32)
    ext_ref[SUBLANES:, GDN_KEY:2 * GDN_KEY] = k_ref[...].astype(F32)
    ext_ref[SUBLANES:, 2 * GDN_KEY:] = v_ref[...].astype(F32)

    first = SUBLANES - (SHORT_CONV - 1)
    conv = ext_ref[pl.ds(first, tb), :] * cw_ref[0:1, :]
    for t in range(1, SHORT_CONV):
        conv = conv + ext_ref[pl.ds(first + t, tb), :] * cw_ref[t:t + 1, :]
    qkv = conv * jax.nn.sigmoid(conv)

    ab = ab_ref[...]
    g = -jnp.exp(alog_ref[...]) * jax.nn.softplus(ab + dtb_ref[...])
    beta = jax.nn.sigmoid(ab)

    rr = lax.broadcasted_iota(jnp.int32, (tb, tb), 0)
    cc = lax.broadcasted_iota(jnp.int32, (tb, tb), 1)
    csh = c_.bit_length() - 1
    tri = ((rr >= cc) & (jnp.right_shift(rr, csh) == jnp.right_shift(cc, csh))).astype(BF16)
    g_hi = g.astype(BF16)
    g_r1 = g - g_hi.astype(F32)
    g_mid = g_r1.astype(BF16)
    g_lo = (g_r1 - g_mid.astype(F32)).astype(BF16)
    gc = (jnp.dot(tri, g_hi, preferred_element_type=F32)
          + jnp.dot(tri, g_mid, preferred_element_type=F32)
          + jnp.dot(tri, g_lo, preferred_element_type=F32))
    qkv_ref[...] = qkv
    gc_ref[...] = gc
    gct_ref[...] = gc.T
    beta_ref[...] = beta

    row = lax.broadcasted_iota(jnp.int32, (c_, c_), 0)
    col = lax.broadcasted_iota(jnp.int32, (c_, c_), 1)
    incl = row >= col
    strict = row > col

    pairs = [(c, h) for c in range(nc) for h in range(GDN_HEADS)]

    def stack(fn):
        return jnp.stack([fn(c * c_, h) for c, h in pairs])

    q = stack(lambda r0, h: qkv_ref[r0:r0 + c_, h * GDN_DK:(h + 1) * GDN_DK])
    k = stack(lambda r0, h: qkv_ref[r0:r0 + c_, GDN_KEY + h * GDN_DK:GDN_KEY + (h + 1) * GDN_DK])
    v = stack(lambda r0, h: qkv_ref[r0:r0 + c_, 2 * GDN_KEY + h * GDN_DK:2 * GDN_KEY + (h + 1) * GDN_DK])
    gcol = stack(lambda r0, h: gc_ref[r0:r0 + c_, h:h + 1])
    grow = stack(lambda r0, h: gct_ref[h:h + 1, r0:r0 + c_])
    bcol = stack(lambda r0, h: beta_ref[r0:r0 + c_, GDN_HEADS + h:GDN_HEADS + h + 1])
    glast = gcol[:, c_ - 1:c_, :]

    q = q * (lax.rsqrt(jnp.sum(q * q, axis=-1, keepdims=True) + RMS_EPS) * (GDN_DK ** -0.5))
    k = k * lax.rsqrt(jnp.sum(k * k, axis=-1, keepdims=True) + RMS_EPS)
    decay = jnp.where(incl, jnp.exp(gcol - grow), 0.0)
    egc = jnp.exp(gcol)
    kb = k * bcol
    low = jnp.where(strict, _bmm_nt(kb, k) * decay, 0.0)
    tinv = _unit_lower_inverse(low, row, col)
    uw = _bmm(tinv, jnp.concatenate([v * bcol, kb * egc], axis=-1))
    qk = _bmm_nt(q, k) * decay
    ktt = jnp.swapaxes(k * jnp.exp(glast - gcol), 1, 2)
    lhs_state = jnp.concatenate([uw[:, :, GDN_DK:], q * egc], axis=1)
    lhs_vnew = jnp.concatenate([qk, ktt], axis=1)
    u = uw[:, :, :GDN_DK]
    sdecay = jnp.exp(glast)

    s = s_ref[...]
    for c in range(nc):
        sl = slice(c * GDN_HEADS, (c + 1) * GDN_HEADS)
        from_state = _bmm(lhs_state[sl], s)
        v_new = u[sl] - from_state[:, :c_]
        from_vnew = _bmm(lhs_vnew[sl], v_new)
        o = from_state[:, c_:] + from_vnew[:, :c_]
        s = s * sdecay[sl] + from_vnew[:, c_:]
        on = o * lax.rsqrt(jnp.mean(o * o, axis=-1, keepdims=True) + RMS_EPS) * onw_ref[...]
        r0 = c * c_
        for h in range(GDN_HEADS):
            zh = z_ref[r0:r0 + c_, h * GDN_DK:(h + 1) * GDN_DK].astype(F32)
            o_ref[r0:r0 + c_, h * GDN_DK:(h + 1) * GDN_DK] = (
                on[h] * (zh * jax.nn.sigmoid(zh))).astype(o_ref.dtype)
    s_ref[...] = s


def _gdn(p, ab, conv_w, alog_row, dtb_row, onorm_row, tb):
    L = p.shape[0]
    row = lambda i: (0, 0)
    return pl.pallas_call(
        _gdn_kernel,
        grid=(L // tb,),
        in_specs=[
            pl.BlockSpec((tb, GDN_KEY), lambda i: (i, 2)),
            pl.BlockSpec((tb, GDN_KEY), lambda i: (i, 3)),
            pl.BlockSpec((tb, GDN_KEY), lambda i: (i, 4)),
            pl.BlockSpec((tb, GDN_KEY), lambda i: (i, 5)),
            pl.BlockSpec((tb, LANES), lambda i: (i, 0)),
            pl.BlockSpec((SUBLANES, 3 * GDN_KEY), row),
            pl.BlockSpec((1, LANES), row),
            pl.BlockSpec((1, LANES), row),
            pl.BlockSpec((1, LANES), row),
        ],
        out_specs=pl.BlockSpec((tb, GDN_KEY), lambda i: (i, 0)),
        out_shape=jax.ShapeDtypeStruct((L, GDN_KEY), BF16),
        scratch_shapes=[pltpu.VMEM((tb + SUBLANES, 3 * GDN_KEY), F32),
                        pltpu.VMEM((GDN_HEADS, GDN_DK, GDN_DK), F32),
                        pltpu.VMEM((tb, 3 * GDN_KEY), F32),
                        pltpu.VMEM((tb, LANES), F32),
                        pltpu.VMEM((LANES, tb), F32),
                        pltpu.VMEM((tb, LANES), F32)],
        compiler_params=_cparams("arbitrary"),
        name="gated_deltanet",
    )(p, p, p, p, ab, conv_w, alog_row, dtb_row, onorm_row)


def _ffn_kernel(x_ref, ya_ref, yb_ref, wo_ref, nw_ref, wg_ref, wv_ref, cw_ref, wd_ref, fnw_ref,
                o_ref, h_ref, acc_ref, ubuf_a, ubuf_b, carry_ref, *, final_norm):
    tm = x_ref.shape[0]
    kh = ya_ref.shape[1]
    nv = tm // SUBLANES
    pitch = nv + SUBLANES
    halo = (FFN_CONV - 1) * SUBLANES
    i = pl.program_id(0)

    @pl.when(i == 0)
    def _():
        carry_ref[...] = jnp.zeros(carry_ref.shape, F32)

    x = (x_ref[...] + jnp.dot(ya_ref[...], wo_ref[0:kh, :], preferred_element_type=F32)
         + jnp.dot(yb_ref[...], wo_ref[kh:, :], preferred_element_type=F32))
    o_ref[...] = x
    ms = jnp.mean(x * x, axis=-1, keepdims=True)
    hn = x * lax.rsqrt(ms + RMS_EPS) * nw_ref[...]
    n_slab = acc_ref.shape[0]
    for s in range(SUBLANES):
        for l in range(n_slab):
            acc_ref[l, pitch * s:pitch * s + nv, :] = hn[nv * s:nv * (s + 1), l * LANES:(l + 1) * LANES]
    for v in range(0, nv, 2):
        for l in range(n_slab):
            pair = jnp.concatenate([acc_ref[l, pl.ds(v, SUBLANES, stride=pitch), :],
                                    acc_ref[l, pl.ds(v + 1, SUBLANES, stride=pitch), :]], axis=0)
            h_ref[SUBLANES * v:SUBLANES * (v + 2), l * LANES:(l + 1) * LANES] = pair.astype(BF16)
    acc_ref[:, 0:tm, :] = jnp.zeros((n_slab, tm, LANES), F32)
    sub = lax.broadcasted_iota(jnp.int32, (SUBLANES, FF_CHUNK), 0)

    def up(c, ubuf):
        h = h_ref[...]
        for gv, w_ref in enumerate((wg_ref, wv_ref)):
            u = jnp.dot(h, w_ref[c], preferred_element_type=F32)
            prev = carry_ref[c, gv]
            for g in range(FFN_CONV - 1):
                lo = tm - halo + SUBLANES * g
                wrapped = pltpu.roll(u[lo:lo + SUBLANES, :], 1, 0)
                top = prev[SUBLANES * g + SUBLANES - 1:SUBLANES * (g + 1), :]
                ubuf[gv, SUBLANES * g:SUBLANES * (g + 1), :] = jnp.where(sub == 0, top, wrapped)
            ubuf[gv, halo:, :] = u
            carry_ref[c, gv] = u[tm - halo:, :]

    def down(c, ubuf):
        cw = cw_ref[c]
        gate, val = [
            functools.reduce(jnp.add, [
                ubuf[gv, pl.ds(SUBLANES * t, tm), :] * cw[gv, t:t + 1, :]
                for t in range(FFN_CONV)])
            for gv in range(2)]
        act = (gate * jax.nn.sigmoid(gate) * val).astype(BF16)
        d = jnp.dot(act, wd_ref[c], preferred_element_type=F32)
        for l in range(n_slab):
            acc_ref[l, 0:tm, :] += d[:, l * LANES:(l + 1) * LANES]

    ubufs = (ubuf_a, ubuf_b)
    up(0, ubufs[0])
    for c in range(N_FF_CHUNKS):
        if c + 1 < N_FF_CHUNKS:
            up(c + 1, ubufs[(c + 1) % 2])
        down(c, ubufs[c % 2])
    runs = nv // SUBLANES
    for j in range(nv):
        first = SUBLANES * SUBLANES * (j % runs) + j // runs
        rows = slice(SUBLANES * j, SUBLANES * (j + 1))
        y = o_ref[rows, :] + jnp.concatenate(
            [acc_ref[l, pl.ds(first, SUBLANES, stride=SUBLANES), :] for l in range(n_slab)], axis=1)
        if final_norm:
            ms2 = jnp.mean(y * y, axis=-1, keepdims=True)
            y = y * lax.rsqrt(ms2 + RMS_EPS) * fnw_ref[...]
        o_ref[rows, :] = y


def _ffn(x, ya, ca, yb, cb, w_out, nw, wu, layer, cw, wd, fnw, tm, final_norm):
    L, D = x.shape
    kh = w_out.shape[0] // 2
    halo = (FFN_CONV - 1) * SUBLANES
    assert tm % (SUBLANES * SUBLANES) == 0
    const2 = lambda i: (0, 0)
    resident = dict(pipeline_mode=pl.Buffered(1))
    return pl.pallas_call(
        functools.partial(_ffn_kernel, final_norm=final_norm),
        grid=(L // tm,),
        in_specs=[
            pl.BlockSpec((tm, D), lambda i: (i, 0)),
            pl.BlockSpec((tm, kh), lambda i: (i, ca)),
            pl.BlockSpec((tm, kh), lambda i: (i, cb)),
            pl.BlockSpec((2 * kh, D), const2, **resident),
            pl.BlockSpec((1, D), const2),
            pl.BlockSpec((None, N_FF_CHUNKS, D, FF_CHUNK), lambda i: (layer, 0, 0, 0), **resident),
            pl.BlockSpec((None, N_FF_CHUNKS, D, FF_CHUNK), lambda i: (layer, 1, 0, 0), **resident),
            pl.BlockSpec((N_FF_CHUNKS, 2, SUBLANES, FF_CHUNK), lambda i: (0, 0, 0, 0)),
            pl.BlockSpec((None, N_FF_CHUNKS, FF_CHUNK, D), lambda i: (layer, 0, 0, 0), **resident),
            pl.BlockSpec((1, D), const2),
        ],
        out_specs=pl.BlockSpec((tm, D), lambda i: (i, 0)),
        out_shape=jax.ShapeDtypeStruct((L, D), F32),
        scratch_shapes=[pltpu.VMEM((tm, D), BF16),
                        pltpu.VMEM((D // LANES, tm + SUBLANES * SUBLANES, LANES), F32),
                        pltpu.VMEM((2, tm + halo, FF_CHUNK), F32),
                        pltpu.VMEM((2, tm + halo, FF_CHUNK), F32),
                        pltpu.VMEM((N_FF_CHUNKS, 2, halo, FF_CHUNK), F32)],
        compiler_params=_cparams("arbitrary"),
        name="conv_ffn",
    )(x, ya, yb, w_out, nw, wu, wu, cw, wd, fnw)


def _chunk_weights_kernel(w_ref, o_ref):
    for g in range(o_ref.shape[1]):
        o_ref[0, g] = w_ref[0, :, g * FF_CHUNK:(g + 1) * FF_CHUNK].astype(o_ref.dtype)


def _chunk_up_weights(w_up, per_step=2):
    nl, d, n = w_up.shape
    nchunk = n // FF_CHUNK
    return pl.pallas_call(
        _chunk_weights_kernel,
        grid=(nl, nchunk // per_step),
        in_specs=[pl.BlockSpec((1, d, per_step * FF_CHUNK), lambda l, c: (l, 0, c))],
        out_specs=pl.BlockSpec((1, per_step, d, FF_CHUNK), lambda l, c: (l, c, 0, 0)),
        out_shape=jax.ShapeDtypeStruct((nl, nchunk, d, FF_CHUNK), BF16),
        compiler_params=_cparams("arbitrary", "arbitrary"),
        name="chunk_up_weights",
    )(w_up)


def _softplus2(z):
    return jnp.maximum(jnp.log2(1.0 + jnp.exp2(jnp.minimum(z, 126.0))), z)


def _attn_kernel(q_ref, k_ref, v_ref, u_ref, o_ref):
    tq = q_ref.shape[0]
    tk = u_ref.shape[0]
    qi = pl.program_id(1)
    n_diag = tq // tk
    n_pair = q_ref.shape[1] // LANES
    lane = lax.broadcasted_iota(jnp.int32, (1, LANES), 1)
    umat = u_ref[...]
    rows = 2 * tq
    rowq = jnp.bitwise_and(lax.broadcasted_iota(jnp.int32, (rows, tk), 0), tq - 1)
    colk = lax.broadcasted_iota(jnp.int32, (rows, tk), 1)

    qs = []
    for p in range(n_pair):
        q = q_ref[:, p * LANES:(p + 1) * LANES]
        zero = jnp.zeros_like(q)
        qs.append(jnp.concatenate([jnp.where(lane < SB_HD, q, zero),
                                   jnp.where(lane >= SB_HD, q, zero)], axis=0))

    def tile(kb, runs, accs, diag_offset, bias=None):
        start = pl.multiple_of(kb * tk, tk)
        zs, sps = [], []
        for p in range(n_pair):
            k = k_ref[pl.ds(start, tk), p * LANES:(p + 1) * LANES]
            z = lax.dot_general(qs[p], k, (((1,), (1,)), ((), ())), preferred_element_type=F32)
            if bias is not None:
                z = z + bias
            sp = _softplus2(z)
            if diag_offset is not None:
                valid = (colk + diag_offset) < rowq
                sp = jnp.where(valid, sp, 0.0)
            zs.append(z)
            sps.append(sp.astype(BF16))
        cum_all = jnp.dot(jnp.concatenate(sps, axis=0), umat, preferred_element_type=F32)
        new_runs, new_accs = [], []
        for p in range(n_pair):
            v = v_ref[pl.ds(start, tk), p * LANES:(p + 1) * LANES]
            cum = cum_all[p * rows:(p + 1) * rows]
            att = jnp.exp2(zs[p] - cum - runs[p])
            if diag_offset is not None:
                att = jnp.where(valid, att, 0.0)
            new_accs.append(accs[p] + jnp.dot(att.astype(BF16), v, preferred_element_type=F32))
            new_runs.append(runs[p] + cum[:, 0:1])
        return tuple(new_runs), tuple(new_accs)

    assert tq & (tq - 1) == 0
    runs = tuple(jnp.zeros((rows, 1), F32) for _ in range(n_pair))
    accs = tuple(jnp.zeros((rows, LANES), F32) for _ in range(n_pair))
    for d in range(n_diag):
        dd = n_diag - 1 - d
        runs, accs = tile(qi * n_diag + dd, runs, accs, dd * tk)
    has_left = qi * n_diag >= 1
    runs, accs = tile(jnp.maximum(qi * n_diag - 1, 0), runs, accs, None,
                      bias=jnp.where(has_left, 0.0, ATT_OFF).astype(F32))
    kb0 = qi * n_diag - 2

    def cond(state):
        kb, runs, _ = state
        least = functools.reduce(jnp.minimum, runs)
        return (kb >= 0) & (jnp.min(least) < ATT_STOP)

    def body(state):
        kb, runs, accs = state
        runs, accs = tile(kb, runs, accs, None)
        return kb - 1, runs, accs

    _, _, accs = lax.while_loop(cond, body, (kb0, runs, accs))
    for p in range(n_pair):
        o_ref[:, p * LANES:(p + 1) * LANES] = jnp.where(
            lane < SB_HD, accs[p][:tq], accs[p][tq:]).astype(o_ref.dtype)


def _attention(qkv, umat, tq, pairs_per_step):
    L = qkv.shape[0]
    n_pairs = SB_HEADS * SB_HD // LANES
    n_steps = n_pairs // pairs_per_step
    width = pairs_per_step * LANES
    tk = umat.shape[0]
    return pl.pallas_call(
        _attn_kernel,
        grid=(n_steps, L // tq),
        in_specs=[
            pl.BlockSpec((tq, width), lambda hp, i: (i, hp)),
            pl.BlockSpec((L, width), lambda hp, i: (0, n_steps + hp), pipeline_mode=pl.Buffered(1)),
            pl.BlockSpec((L, width), lambda hp, i: (0, 2 * n_steps + hp), pipeline_mode=pl.Buffered(1)),
            pl.BlockSpec((tk, tk), lambda hp, i: (0, 0)),
        ],
        out_specs=pl.BlockSpec((tq, width), lambda hp, i: (i, hp)),
        out_shape=jax.ShapeDtypeStruct((L, SB_HEADS * SB_HD), BF16),
        compiler_params=_cparams("arbitrary", "arbitrary"),
        name="stickbreak_attention",
    )(qkv, qkv, qkv, umat)


def _row(v, width=None):
    v = v.astype(F32).reshape(1, -1)
    if width is not None and v.shape[1] < width:
        v = jnp.pad(v, ((0, 0), (0, width - v.shape[1])))
    return v


def _pad_rows(w, rows):
    return jnp.pad(w.astype(F32), ((0, rows - w.shape[0]), (0, 0)))


def _ffn_conv_weights(conv_w):
    return _pad_rows(conv_w, SUBLANES).reshape(SUBLANES, 2, N_FF_CHUNKS, FF_CHUNK).transpose(2, 1, 0, 3)


def kernel(x, mix0_norm, w_in0, a_dw_w, a_dw_b, a_ln_w, a_ln_b, gdn_conv_w, gdn_a_log,
           gdn_dt_bias, gdn_onorm_w, w_out0, mix1_norm, w_qkv1, w_out1,
           ffn_norm, w_up, ffn_conv_w, w_down, final_norm):
    B, L, D = x.shape
    assert B == 1 and D == D_MODEL
    xs = x.reshape(L, D)

    w_in = w_in0[0].astype(BF16)
    w_ab = jnp.pad(w_in[:, IN0_MAIN:], ((0, 0), (0, LANES - (w_in.shape[1] - IN0_MAIN))))
    p, ab = _rms_matmul(xs, _row(mix0_norm[0]), w_in, jnp.ones((1, IN0_MAIN), F32), BF16,
                        tm=PROJ_TM, tn=PROJ_TN, w_side=w_ab)
    wd = w_down.astype(BF16).reshape(w_down.shape[0], N_FF_CHUNKS, FF_CHUNK, D)
    wu = _chunk_up_weights(w_up, per_step=N_FF_CHUNKS)
    y_a = _conformer(p, _pad_rows(a_dw_w[0], CONF_HALO), _row(a_dw_b[0]), _row(a_ln_w[0]),
                     _row(a_ln_b[0]), tm=CONF_TM)
    y_b = _gdn(p, ab, _pad_rows(gdn_conv_w[0], SUBLANES), _row(gdn_a_log[0], LANES),
               _row(gdn_dt_bias[0], LANES), _row(gdn_onorm_w[0]), tb=GDN_TB)
    cw = _ffn_conv_weights(ffn_conv_w[0])
    xs = _ffn(xs, y_a, 0, y_b, 0, w_out0[0].astype(BF16), _row(ffn_norm[0]), wu, 0, cw, wd,
              _row(final_norm), tm=FFN_TM, final_norm=False)

    hd = SB_HEADS * SB_HD
    qscale = jnp.concatenate([jnp.full((1, hd), SB_HD ** -0.5 * LOG2E, F32),
                              jnp.ones((1, 2 * hd), F32)], axis=1)
    qkv = _rms_matmul(xs, _row(mix1_norm[0]), w_qkv1[0].astype(BF16), qscale, BF16, tm=PROJ_TM,
                      tn=PROJ_TN)
    ki = lax.broadcasted_iota(jnp.int32, (ATT_TK, ATT_TK), 0)
    si = lax.broadcasted_iota(jnp.int32, (ATT_TK, ATT_TK), 1)
    umat = (ki >= si).astype(BF16)
    o = _attention(qkv, umat, tq=ATT_TQ, pairs_per_step=ATT_PAIRS)
    cw = _ffn_conv_weights(ffn_conv_w[1])
    xs = _ffn(xs, o, 0, o, 1, w_out1[0].astype(BF16), _row(ffn_norm[1]), wu, 1, cw, wd,
              _row(final_norm), tm=FFN_TM, final_norm=True)
    return xs.reshape(B, L, D)
```
